```python
import math
import jax, jax.numpy as jnp
from jax import lax
import numpy as np

D_MODEL = 1024
BATCH = 16
SEQ = 4096
DEPTH = 4

N_MEM = 256
GRID_W = 64
EPS = 1e-6
FOURIER_GROUPS = 4
FOURIER_GROUP_DIM = D_MODEL // 8
FOURIER_WIDTH = FOURIER_GROUPS * FOURIER_GROUP_DIM
NA_HEADS = 4
NA_HEAD_DIM = D_MODEL // 8
NA_WIDTH = NA_HEADS * NA_HEAD_DIM
NA_KH = 8
NA_KW = 16
AB_IN_WIDTH = FOURIER_WIDTH + 3 * NA_WIDTH
AB_OUT_WIDTH = FOURIER_WIDTH + NA_WIDTH
CONV_WIDTH = 3
XA_HEADS = 4
XA_HEAD_DIM = D_MODEL // XA_HEADS
D_FF = 2 * D_MODEL
N_EVEN = (DEPTH + 1) // 2
N_ODD = DEPTH // 2

kernel_name = "hybrid_fourier_natten_shortconv_encoder"


def rms_norm(x, g):
    xf = x.astype(jnp.float32)
    y = xf * lax.rsqrt(jnp.mean(xf * xf, axis=-1, keepdims=True) + EPS)
    return (y * g.astype(jnp.float32)).astype(x.dtype)


def dwconv3_centred(x, w):
    xp = jnp.pad(x, ((0, 0), (1, 1), (0, 0)))
    return xp[:, :-2] * w[0] + xp[:, 1:-1] * w[1] + xp[:, 2:] * w[2]


def fourier_mix(u):
    f = jnp.fft.fft2(u.astype(jnp.float32), axes=(1, 3), norm="ortho")
    return jnp.real(f).astype(u.dtype)


def neighborhood_attention(q, k, v, rpb):
    B, S, H, dh = q.shape
    rows = S // GRID_W
    kh = min(NA_KH, rows)
    kw = NA_KW
    qg = q.reshape(B, rows, GRID_W, H, dh)
    kg = k.reshape(B, rows, GRID_W, H, dh)
    vg = v.reshape(B, rows, GRID_W, H, dh)
    cols = jnp.arange(GRID_W)
    col_start = jnp.clip(cols - kw // 2, 0, GRID_W - kw)
    col_idx = col_start[:, None] + jnp.arange(kw)[None, :]
    col_off = col_idx - cols[:, None] + (NA_KW - 1)
    scale = dh ** -0.5

    def one_row(i):
        rs = jnp.clip(i - kh // 2, 0, rows - kh)
        q_i = lax.dynamic_index_in_dim(qg, i, axis=1, keepdims=False)
        k_blk = lax.dynamic_slice_in_dim(kg, rs, kh, axis=1)
        v_blk = lax.dynamic_slice_in_dim(vg, rs, kh, axis=1)
        k_win = k_blk[:, :, col_idx]
        v_win = v_blk[:, :, col_idx]
        s = jnp.einsum('bjhd,brjchd->bhjrc', q_i, k_win).astype(jnp.float32) * scale
        row_off = rs + jnp.arange(kh) - i + (NA_KH - 1)
        bias = rpb[:, row_off][:, :, col_off]
        s = s + jnp.transpose(bias, (0, 2, 1, 3)).astype(jnp.float32)[None]
        p = jax.nn.softmax(s.reshape(B, H, GRID_W, kh * kw), axis=-1)
        p = p.reshape(B, H, GRID_W, kh, kw).astype(v.dtype)
        return jnp.einsum('bhjrc,brjchd->bjhd', p, v_win)

    out = lax.map(one_row, jnp.arange(rows))
    return jnp.moveaxis(out, 0, 1).reshape(B, S, H, dh)


def fourier_na_mixer(h, w_in, rpb, w_out):
    B, S, _ = h.shape
    z = h @ w_in
    zf = z[..., :FOURIER_WIDTH]
    q = z[..., FOURIER_WIDTH:FOURIER_WIDTH + NA_WIDTH].reshape(B, S, NA_HEADS, NA_HEAD_DIM)
    k = z[..., FOURIER_WIDTH + NA_WIDTH:FOURIER_WIDTH + 2 * NA_WIDTH].reshape(B, S, NA_HEADS, NA_HEAD_DIM)
    v = z[..., FOURIER_WIDTH + 2 * NA_WIDTH:].reshape(B, S, NA_HEADS, NA_HEAD_DIM)
    yf = fourier_mix(zf.reshape(B, S, FOURIER_GROUPS, FOURIER_GROUP_DIM)).reshape(B, S, FOURIER_WIDTH)
    ya = neighborhood_attention(q, k, v, rpb).reshape(B, S, NA_WIDTH)
    return jnp.concatenate([yf, ya], axis=-1) @ w_out


def short_gated_conv_mixer(h, w_in, conv_w, w_out):
    z = h @ w_in
    gate_b = z[..., :D_MODEL]
    gate_c = z[..., D_MODEL:2 * D_MODEL]
    u = z[..., 2 * D_MODEL:]
    return (gate_b * dwconv3_centred(gate_c * u, conv_w)) @ w_out


def memory_cross_attention(h, m, wq, wkv, wo):
    B, S, D = h.shape
    q = (h @ wq).reshape(B, S, XA_HEADS, XA_HEAD_DIM)
    kv = m @ wkv
    k = kv[..., :D].reshape(B, -1, XA_HEADS, XA_HEAD_DIM)
    v = kv[..., D:].reshape(B, -1, XA_HEADS, XA_HEAD_DIM)
    s = jnp.einsum('bshd,bmhd->bhsm', q, k).astype(jnp.float32) * (XA_HEAD_DIM ** -0.5)
    p = jax.nn.softmax(s, axis=-1).astype(v.dtype)
    o = jnp.einsum('bhsm,bmhd->bshd', p, v).reshape(B, S, D)
    return o @ wo


def conv_ffn(h, w_up, conv_w, conv_b, w_down):
    z = h @ w_up
    u = z[..., :D_FF]
    g = dwconv3_centred(z[..., D_FF:], conv_w) + conv_b
    return (jax.nn.gelu(g, approximate=False) * u) @ w_down


def setup_inputs(seed: int = 0) -> dict:
    key = jax.random.key(seed)
    ks = jax.random.split(key, 24)
    f32 = jnp.float32
    nrm = lambda k, shape, fan_in: jax.random.normal(k, shape, f32) * (fan_in ** -0.5)
    gain = lambda k, shape: 1.0 + 0.1 * jax.random.normal(k, shape, f32)
    D = D_MODEL
    return {
        "x": jax.random.normal(ks[0], (BATCH, SEQ, D), f32),
        "mem": jax.random.normal(ks[1], (BATCH, N_MEM, D), f32),
        "mem_norm_g": gain(ks[2], (D,)),
        "mix_norm_g": gain(ks[3], (DEPTH, D)),
        "w_in_ab": nrm(ks[4], (N_EVEN, D, AB_IN_WIDTH), D),
        "rpb": 0.02 * jax.random.normal(ks[5], (N_EVEN, NA_HEADS, 2 * NA_KH - 1, 2 * NA_KW - 1), f32),
        "w_out_ab": nrm(ks[6], (N_EVEN, AB_OUT_WIDTH, D), AB_OUT_WIDTH),
        "w_in_c": nrm(ks[7], (N_ODD, D, 3 * D), D),
        "conv_c": nrm(ks[8], (N_ODD, CONV_WIDTH, D), CONV_WIDTH),
        "w_out_c": nrm(ks[9], (N_ODD, D, D), D),
        "xa_norm_g": gain(ks[10], (DEPTH, D)),
        "xa_wq": nrm(ks[11], (DEPTH, D, D), D),
        "xa_wkv": nrm(ks[12], (DEPTH, D, 2 * D), D),
        "xa_wo": nrm(ks[13], (DEPTH, D, D), D),
        "ffn_norm_g": gain(ks[14], (DEPTH, D)),
        "ffn_w_up": nrm(ks[15], (DEPTH, D, 2 * D_FF), D),
        "ffn_conv_w": nrm(ks[16], (DEPTH, CONV_WIDTH, D_FF), CONV_WIDTH),
        "ffn_conv_b": 0.02 * jax.random.normal(ks[17], (DEPTH, D_FF), f32),
        "ffn_w_down": nrm(ks[18], (DEPTH, D_FF, D), D_FF),
        "final_norm_g": gain(ks[19], (D,)),
    }


def reference(x, mem, mem_norm_g, mix_norm_g, w_in_ab, rpb, w_out_ab, w_in_c, conv_c, w_out_c,
              xa_norm_g, xa_wq, xa_wkv, xa_wo, ffn_norm_g, ffn_w_up, ffn_conv_w, ffn_conv_b,
              ffn_w_down, final_norm_g):
    m = rms_norm(mem, mem_norm_g)
    h = x
    for layer in range(DEPTH):
        hn = rms_norm(h, mix_norm_g[layer])
        if layer % 2 == 0:
            j = layer // 2
            h = h + fourier_na_mixer(hn, w_in_ab[j], rpb[j], w_out_ab[j])
        else:
            j = layer // 2
            h = h + short_gated_conv_mixer(hn, w_in_c[j], conv_c[j], w_out_c[j])
        h = h + memory_cross_attention(rms_norm(h, xa_norm_g[layer]), m,
                                       xa_wq[layer], xa_wkv[layer], xa_wo[layer])
        h = h + conv_ffn(rms_norm(h, ffn_norm_g[layer]), ffn_w_up[layer],
                         ffn_conv_w[layer], ffn_conv_b[layer], ffn_w_down[layer])
    return rms_norm(h, final_norm_g)
```

```python
import functools
import math

import numpy as np
import jax
import jax.numpy as jnp
from jax import lax
from jax.experimental import pallas as pl
from jax.experimental.pallas import tpu as pltpu

EPS = 1e-6
GRID_W = 64
NA_KH, NA_KW = 8, 16
NA_HEADS = 4
NA_HEAD_DIM = 128
FOURIER_GROUPS = 4
FOURIER_GROUP_DIM = 128
XA_HEADS = 4
NEG_MASK = -1e30

HALO = 16
VMEM_LIMIT = 56 * 1024 * 1024

F32 = jnp.float32
BF16 = jnp.bfloat16


def _rms(x, g):
    ms = jnp.mean(x * x, axis=-1, keepdims=True)
    return (x * lax.rsqrt(ms + EPS)) * g


def _dot(a, b):
    return jnp.dot(a, b, preferred_element_type=F32)


def _dot_nt(a, b):
    return lax.dot_general(a, b, (((1,), (1,)), ((), ())), preferred_element_type=F32)


def _softmax_pv(s, v):
    m = jnp.max(s, axis=-1, keepdims=True)
    p = jnp.exp(s - m)
    l = jnp.sum(p, axis=-1, keepdims=True)
    return _dot(p.astype(BF16), v) / l


def _const_spec(shape):
    nd = len(shape)
    return pl.BlockSpec(shape, lambda *_: (0,) * nd, pipeline_mode=pl.Buffered(1))


def _params(*sem):
    return pltpu.CompilerParams(dimension_semantics=sem, vmem_limit_bytes=VMEM_LIMIT)


def _norm_proj_kernel(x_ref, g_ref, w_ref, o_ref):
    hn = _rms(x_ref[...], g_ref[...]).astype(BF16)
    o_ref[0] = _dot(hn, w_ref[0]).astype(o_ref.dtype)


def _norm_proj(x2d, g, w_stack, tm):
    m, d = x2d.shape
    nl, _, n = w_stack.shape
    return pl.pallas_call(
        _norm_proj_kernel,
        grid=(nl, m // tm),
        in_specs=[
            pl.BlockSpec((tm, d), lambda l, i: (i, 0)),
            pl.BlockSpec((1, d), lambda l, i: (0, 0)),
            pl.BlockSpec((1, d, n), lambda l, i: (l, 0, 0)),
        ],
        out_specs=pl.BlockSpec((1, tm, n), lambda l, i: (l, i, 0)),
        out_shape=jax.ShapeDtypeStruct((nl, m, n), BF16),
        compiler_params=_params("arbitrary", "arbitrary"),
        name="mem_kv_proj",
    )(x2d, g.reshape(1, d), w_stack)


def _even_in_kernel(x_ref, g_ref, w_ref, cdft_ref, pq_ref, qkv_ref):
    fw = FOURIER_GROUPS * FOURIER_GROUP_DIM
    nw = NA_HEADS * NA_HEAD_DIM
    hn = _rms(x_ref[...], g_ref[...]).astype(BF16)
    z = _dot(hn, w_ref[...])
    q = z[:, fw:fw + nw] * (NA_HEAD_DIM ** -0.5)
    qkv_ref[:, :nw] = q.astype(BF16)
    qkv_ref[:, nw:] = z[:, fw + nw:].astype(BF16)
    cd = cdft_ref[...]
    gd = FOURIER_GROUP_DIM
    for grp in range(FOURIER_GROUPS):
        zf = z[:, grp * gd:(grp + 1) * gd].astype(BF16)
        pq = _dot(zf, cd)
        pq_ref[:, grp * gd:(grp + 1) * gd] = pq[:, :gd].astype(BF16)
        pq_ref[:, fw + grp * gd:fw + (grp + 1) * gd] = pq[:, gd:].astype(BF16)


def _even_in(h2d, g, w_in, cdft, tm):
    m, d = h2d.shape
    n = w_in.shape[1]
    fw = FOURIER_GROUPS * FOURIER_GROUP_DIM
    return pl.pallas_call(
        _even_in_kernel,
        grid=(m // tm,),
        in_specs=[
            pl.BlockSpec((tm, d), lambda i: (i, 0)),
            _const_spec((1, d)),
            _const_spec((d, n)),
            _const_spec(cdft.shape),
        ],
        out_specs=[
            pl.BlockSpec((tm, 2 * fw), lambda i: (i, 0)),
            pl.BlockSpec((tm, n - fw), lambda i: (i, 0)),
        ],
        out_shape=[
            jax.ShapeDtypeStruct((m, 2 * fw), BF16),
            jax.ShapeDtypeStruct((m, n - fw), BF16),
        ],
        compiler_params=_params("arbitrary"),
        name="even_in_proj",
    )(h2d, g.reshape(1, d), w_in, cdft)


def _seq_dft_kernel(m_ref, pq_ref, o_ref):
    s = pq_ref.shape[1]
    fw = o_ref.shape[2]
    y = _dot(m_ref[:, :s], pq_ref[0, :, :fw]) + _dot(m_ref[:, s:], pq_ref[0, :, fw:])
    o_ref[0] = y.astype(o_ref.dtype)


def _seq_dft(sdft, pq, tk):
    b, s, w2 = pq.shape
    fw = w2 // 2
    return pl.pallas_call(
        _seq_dft_kernel,
        grid=(b, s // tk),
        in_specs=[
            pl.BlockSpec((tk, 2 * s), lambda bi, k: (k, 0)),
            pl.BlockSpec((1, s, w2), lambda bi, k: (bi, 0, 0)),
        ],
        out_specs=pl.BlockSpec((1, tk, fw), lambda bi, k: (bi, k, 0)),
        out_shape=jax.ShapeDtypeStruct((b, s, fw), BF16),
        compiler_params=_params("arbitrary", "arbitrary"),
        name="seq_dft",
    )(sdft, pq)


NA_QROWS = 4
NA_KROWS = NA_QROWS + NA_KH


def _na_kernel(pat_ref, q_ref, k_ref, v_ref, bias_ref, o_ref, *, rows):
    del pat_ref
    i = pl.program_id(1)
    krow0 = jnp.clip(i * NA_QROWS - NA_KH // 2, 0, rows - NA_KROWS)
    kstart = pl.multiple_of(krow0 * GRID_W, GRID_W)
    nk = NA_KROWS * GRID_W
    dh = NA_HEAD_DIM
    for hd in range(NA_HEADS):
        q = q_ref[0, :, hd * dh:(hd + 1) * dh]
        k = k_ref[0, pl.ds(kstart, nk), hd * dh:(hd + 1) * dh]
        v = v_ref[0, pl.ds(kstart, nk), hd * dh:(hd + 1) * dh]
        s = _dot_nt(q, k) + bias_ref[0, hd]
        o_ref[0, :, hd * dh:(hd + 1) * dh] = _softmax_pv(s, v).astype(o_ref.dtype)


def _na_bias_tables(rpb, rows):
    nblk = rows // NA_QROWS
    patterns, pat_of_block = [], []
    for blk in range(nblk):
        i0 = blk * NA_QROWS
        krow0 = min(max(i0 - NA_KH // 2, 0), rows - NA_KROWS)
        qi = np.arange(i0, i0 + NA_QROWS)
        rs = np.clip(qi - NA_KH // 2, 0, rows - NA_KH)
        key = (krow0 - i0, tuple((rs - i0).tolist()))
        if key not in patterns:
            patterns.append(key)
        pat_of_block.append(patterns.index(key))
    cols = np.arange(GRID_W)
    cs = np.clip(cols - NA_KW // 2, 0, GRID_W - NA_KW)
    col_valid = (cols[None, :] >= cs[:, None]) & (cols[None, :] < cs[:, None] + NA_KW)
    col_off = np.clip(cols[None, :] - cols[:, None] + NA_KW - 1, 0, 2 * NA_KW - 2)
    tables = []
    for dk, drs in patterns:
        a = np.arange(NA_QROWS)
        kr = dk + np.arange(NA_KROWS)
        rs_rel = np.asarray(drs)
        row_valid = (kr[None, :] >= rs_rel[:, None]) & (kr[None, :] < rs_rel[:, None] + NA_KH)
        row_off = np.clip(kr[None, :] - a[:, None] + NA_KH - 1, 0, 2 * NA_KH - 2)
        bias = rpb[:, row_off][:, :, :, col_off]
        bias = jnp.transpose(bias, (0, 1, 3, 2, 4))
        valid = row_valid[:, None, :, None] & col_valid[None, :, None, :]
        bias = jnp.where(valid[None], bias, NEG_MASK)
        tables.append(bias.reshape(rpb.shape[0], NA_QROWS * GRID_W, NA_KROWS * GRID_W))
    return jnp.stack(tables).astype(F32), jnp.asarray(pat_of_block, jnp.int32)


def _neighborhood_attention(qkv, rpb):
    b, s, w3 = qkv.shape
    nw = w3 // 3
    rows = s // GRID_W
    assert rows >= NA_KROWS and rows % NA_QROWS == 0
    tables, pat = _na_bias_tables(rpb, rows)
    tq = NA_QROWS * GRID_W
    nk = NA_KROWS * GRID_W
    grid_spec = pltpu.PrefetchScalarGridSpec(
        num_scalar_prefetch=1,
        grid=(b, rows // NA_QROWS),
        in_specs=[
            pl.BlockSpec((1, tq, nw), lambda bi, i, p: (bi, i, 0)),
            pl.BlockSpec((1, s, nw), lambda bi, i, p: (bi, 0, 1)),
            pl.BlockSpec((1, s, nw), lambda bi, i, p: (bi, 0, 2)),
            pl.BlockSpec((1, NA_HEADS, tq, nk), lambda bi, i, p: (p[i], 0, 0, 0)),
        ],
        out_specs=pl.BlockSpec((1, tq, nw), lambda bi, i, p: (bi, i, 0)),
    )
    return pl.pallas_call(
        functools.partial(_na_kernel, rows=rows),
        grid_spec=grid_spec,
        out_shape=jax.ShapeDtypeStruct((b, s, nw), BF16),
        compiler_params=_params("arbitrary", "arbitrary"),
        name="neighborhood_attn",
    )(pat, qkv, qkv, qkv, tables)


def _even_out_kernel(h_ref, yf_ref, ya_ref, w_ref, o_ref):
    fw = yf_ref.shape[1]
    acc = _dot(yf_ref[...], w_ref[:fw, :]) + _dot(ya_ref[...], w_ref[fw:, :])
    o_ref[...] = h_ref[...] + acc


def _even_out(h2d, yf2d, ya2d, w_out, tm):
    m, d = h2d.shape
    fw, nw = yf2d.shape[1], ya2d.shape[1]
    return pl.pallas_call(
        _even_out_kernel,
        grid=(m // tm,),
        in_specs=[
            pl.BlockSpec((tm, d), lambda i: (i, 0)),
            pl.BlockSpec((tm, fw), lambda i: (i, 0)),
            pl.BlockSpec((tm, nw), lambda i: (i, 0)),
            _const_spec(w_out.shape),
        ],
        out_specs=pl.BlockSpec((tm, d), lambda i: (i, 0)),
        out_shape=jax.ShapeDtypeStruct((m, d), F32),
        compiler_params=_params("arbitrary"),
        name="even_out_proj",
    )(h2d, yf2d, ya2d, w_out)


def _halo_specs(tm, s, d):
    r = tm // HALO
    last_blk = s // HALO - 1

    def prev_map(b, i):
        return (b, jnp.maximum(i * r - 1, 0), 0)

    def next_map(b, i):
        return (b, jnp.minimum((i + 1) * r, last_blk), 0)

    return [
        pl.BlockSpec((1, tm, d), lambda b, i: (b, i, 0)),
        pl.BlockSpec((1, HALO, d), prev_map),
        pl.BlockSpec((1, HALO, d), next_map),
    ]


def _fill_normed_ext(hn_ref, x, xp, xn, g):
    tm = x.shape[0]
    i = pl.program_id(1)
    first = i == 0
    last = i == pl.num_programs(1) - 1
    hp = jnp.where(first, 0.0, _rms(xp, g))
    hx = jnp.where(last, 0.0, _rms(xn, g))
    hn_ref[:HALO, :] = hp.astype(BF16)
    hn_ref[HALO:HALO + tm, :] = _rms(x, g).astype(BF16)
    hn_ref[HALO + tm:, :] = hx.astype(BF16)


def _conv3_from_ref(buf_ref, cw, tm):
    return (buf_ref[pl.ds(HALO - 1, tm), :] * cw[0:1, :]
            + buf_ref[pl.ds(HALO, tm), :] * cw[1:2, :]
            + buf_ref[pl.ds(HALO + 1, tm), :] * cw[2:3, :])


def _odd_mixer_kernel(x_ref, xp_ref, xn_ref, g_ref, win_ref, cw_ref, wout_ref, o_ref,
                      hn_ref, cu_ref, *, nc):
    tm, d = x_ref.shape[1], x_ref.shape[2]
    x = x_ref[0]
    _fill_normed_ext(hn_ref, x, xp_ref[0], xn_ref[0], g_ref[...])
    hn_ext = hn_ref[...]
    hn = hn_ref[HALO:HALO + tm, :]
    acc = x
    for c in range(d // nc):
        lo, hi = c * nc, (c + 1) * nc
        gate_b = _dot(hn, win_ref[:, lo:hi])
        gate_c = _dot(hn_ext, win_ref[:, d + lo:d + hi])
        u = _dot(hn_ext, win_ref[:, 2 * d + lo:2 * d + hi])
        cu_ref[...] = gate_c * u
        y = gate_b * _conv3_from_ref(cu_ref, cw_ref[:, lo:hi], tm)
        acc = acc + _dot(y.astype(BF16), wout_ref[lo:hi, :])
    o_ref[0] = acc


def _odd_mixer(h, g, w_in, conv_w, w_out, tm, nc=512):
    b, s, d = h.shape
    return pl.pallas_call(
        functools.partial(_odd_mixer_kernel, nc=nc),
        grid=(b, s // tm),
        in_specs=_halo_specs(tm, s, d) + [
            _const_spec((1, d)),
            _const_spec(w_in.shape),
            _const_spec(conv_w.shape),
            _const_spec(w_out.shape),
        ],
        out_specs=pl.BlockSpec((1, tm, d), lambda bi, i: (bi, i, 0)),
        out_shape=jax.ShapeDtypeStruct(h.shape, F32),
        scratch_shapes=[
            pltpu.VMEM((tm + 2 * HALO, d), BF16),
            pltpu.VMEM((tm + 2 * HALO, nc), F32),
        ],
        compiler_params=_params("arbitrary", "arbitrary"),
        name="odd_mixer",
    )(h, h, h, g.reshape(1, d), w_in, conv_w, w_out)


def _xattn_kernel(x_ref, g_ref, wq_ref, kv_ref, wo_ref, o_ref, oh_ref):
    d = x_ref.shape[2]
    dh = d // XA_HEADS
    x = x_ref[0]
    hn = _rms(x, g_ref[...]).astype(BF16)
    q = (_dot(hn, wq_ref[...]) * (dh ** -0.5)).astype(BF16)
    for hd in range(XA_HEADS):
        k = kv_ref[0, 0, :, hd * dh:(hd + 1) * dh]
        v = kv_ref[0, 0, :, d + hd * dh:d + (hd + 1) * dh]
        s = _dot_nt(q[:, hd * dh:(hd + 1) * dh], k)
        oh_ref[:, hd * dh:(hd + 1) * dh] = _softmax_pv(s, v).astype(BF16)
    o_ref[0] = x + _dot(oh_ref[...], wo_ref[...])


def _xattn(h, g, wq, kv, layer, wo, tm):
    b, s, d = h.shape
    n_mem = kv.shape[2]
    return pl.pallas_call(
        _xattn_kernel,
        grid=(b, s // tm),
        in_specs=[
            pl.BlockSpec((1, tm, d), lambda bi, i: (bi, i, 0)),
            _const_spec((1, d)),
            _const_spec(wq.shape),
            pl.BlockSpec((1, 1, n_mem, 2 * d), lambda bi, i: (layer, bi, 0, 0)),
            _const_spec(wo.shape),
        ],
        out_specs=pl.BlockSpec((1, tm, d), lambda bi, i: (bi, i, 0)),
        out_shape=jax.ShapeDtypeStruct(h.shape, F32),
        scratch_shapes=[pltpu.VMEM((tm, d), BF16)],
        compiler_params=_params("arbitrary", "arbitrary"),
        name="mem_xattn",
    )(h, g.reshape(1, d), wq, kv, wo)


def _gelu_exact(x):
    return 0.5 * x * (1.0 + lax.erf(x * (1.0 / math.sqrt(2.0))))


def _ffn_kernel(x_ref, xp_ref, xn_ref, g_ref, wup_ref, cw_ref, cb_ref, wdn_ref, gf_ref, o_ref,
                hn_ref, gp_ref, *, nc, final_norm):
    tm = x_ref.shape[1]
    dff = wdn_ref.shape[0]
    x = x_ref[0]
    _fill_normed_ext(hn_ref, x, xp_ref[0], xn_ref[0], g_ref[...])
    hn_ext = hn_ref[...]
    hn = hn_ref[HALO:HALO + tm, :]
    acc = x
    for c in range(dff // nc):
        lo, hi = c * nc, (c + 1) * nc
        u = _dot(hn, wup_ref[:, lo:hi])
        gp_ref[...] = _dot(hn_ext, wup_ref[:, dff + lo:dff + hi])
        gate = _conv3_from_ref(gp_ref, cw_ref[:, lo:hi], tm) + cb_ref[:, lo:hi]
        a = _gelu_exact(gate) * u
        acc = acc + _dot(a.astype(BF16), wdn_ref[lo:hi, :])
    if final_norm:
        acc = _rms(acc, gf_ref[...])
    o_ref[0] = acc


def _conv_ffn(h, g, w_up, conv_w, conv_b, w_down, final_g, tm, nc=512):
    b, s, d = h.shape
    dff = w_down.shape[0]
    final_norm = final_g is not None
    gf = (final_g if final_norm else g).reshape(1, d)
    return pl.pallas_call(
        functools.partial(_ffn_kernel, nc=nc, final_norm=final_norm),
        grid=(b, s // tm),
        in_specs=_halo_specs(tm, s, d) + [
            _const_spec((1, d)),
            _const_spec(w_up.shape),
            _const_spec(conv_w.shape),
            _const_spec((1, dff)),
            _const_spec(w_down.shape),
            _const_spec((1, d)),
        ],
        out_specs=pl.BlockSpec((1, tm, d), lambda bi, i: (bi, i, 0)),
        out_shape=jax.ShapeDtypeStruct(h.shape, F32),
        scratch_shapes=[
            pltpu.VMEM((tm + 2 * HALO, d), BF16),
            pltpu.VMEM((tm + 2 * HALO, nc), F32),
        ],
        compiler_params=_params("arbitrary", "arbitrary"),
        name="conv_ffn",
    )(h, h, h, g.reshape(1, d), w_up, conv_w, conv_b.reshape(1, dff), w_down, gf)


def _dft_cos_sin(n):
    j = jnp.arange(n, dtype=jnp.int32)
    ang = ((j[:, None] * j[None, :]) % n).astype(F32) * (2.0 * math.pi / n)
    return jnp.cos(ang), jnp.sin(ang)


def _dft_constants(s):
    cc, sc = _dft_cos_sin(FOURIER_GROUP_DIM)
    cdft = (jnp.concatenate([cc, sc], axis=1) * (FOURIER_GROUP_DIM ** -0.5)).astype(BF16)
    cs, ss = _dft_cos_sin(s)
    sdft = (jnp.concatenate([cs, -ss], axis=1) * (s ** -0.5)).astype(BF16)
    return cdft, sdft


def kernel(x, mem, mem_norm_g, mix_norm_g, w_in_ab, rpb, w_out_ab, w_in_c, conv_c, w_out_c,
           xa_norm_g, xa_wq, xa_wkv, xa_wo, ffn_norm_g, ffn_w_up, ffn_conv_w, ffn_conv_b,
           ffn_w_down, final_norm_g):
    b, s, d = x.shape
    depth = mix_norm_g.shape[0]
    n_mem = mem.shape[1]
    tm = min(512, s)
    bf = lambda w: w.astype(BF16)

    cdft, sdft = _dft_constants(s)
    kv = _norm_proj(mem.reshape(b * n_mem, d), mem_norm_g, bf(xa_wkv), tm=min(512, b * n_mem))
    kv = kv.reshape(depth, b, n_mem, 2 * d)

    h = x
    for layer in range(depth):
        j = layer // 2
        if layer % 2 == 0:
            pq, qkv = _even_in(h.reshape(b * s, d), mix_norm_g[layer], bf(w_in_ab[j]), cdft, tm)
            yf = _seq_dft(sdft, pq.reshape(b, s, -1), tk=tm)
            ya = _neighborhood_attention(qkv.reshape(b, s, -1), rpb[j])
            h = _even_out(h.reshape(b * s, d), yf.reshape(b * s, -1), ya.reshape(b * s, -1),
                          bf(w_out_ab[j]), tm).reshape(b, s, d)
        else:
            h = _odd_mixer(h, mix_norm_g[layer], bf(w_in_c[j]), conv_c[j], bf(w_out_c[j]), tm)
        h = _xattn(h, xa_norm_g[layer], bf(xa_wq[layer]), kv, layer, bf(xa_wo[layer]), tm)
        h = _conv_ffn(h, ffn_norm_g[layer], bf(ffn_w_up[layer]), ffn_conv_w[layer], ffn_conv_b[layer],
                      bf(ffn_w_down[layer]), final_norm_g if layer == depth - 1 else None, tm)
    return h
```

```python
import functools
import math

import numpy as np
import jax
import jax.numpy as jnp
from jax import lax
from jax.experimental import pallas as pl
from jax.experimental.pallas import tpu as pltpu

EPS = 1e-6
GRID_W = 64
NA_KH, NA_KW = 8, 16
NA_HEADS = 4
NA_HEAD_DIM = 128
FOURIER_GROUPS = 4
FOURIER_GROUP_DIM = 128
XA_HEADS = 4
NEG_MASK = -1e30

HALO = 16
VMEM_LIMIT = 56 * 1024 * 1024

F32 = jnp.float32
BF16 = jnp.bfloat16


def _rms(x, g):
    ms = jnp.mean(x * x, axis=-1, keepdims=True)
    return (x * lax.rsqrt(ms + EPS)) * g


def _dot(a, b):
    return jnp.dot(a, b, preferred_element_type=F32)


def _dot_nt(a, b):
    return lax.dot_general(a, b, (((1,), (1,)), ((), ())), preferred_element_type=F32)


def _softmax_pv(s, v):
    m = jnp.max(s, axis=-1, keepdims=True)
    p = jnp.exp(s - m)
    l = jnp.sum(p, axis=-1, keepdims=True)
    return _dot(p.astype(BF16), v) / l


def _const_spec(shape):
    nd = len(shape)
    return pl.BlockSpec(shape, lambda *_: (0,) * nd, pipeline_mode=pl.Buffered(1))


def _params(*sem):
    return pltpu.CompilerParams(dimension_semantics=sem, vmem_limit_bytes=VMEM_LIMIT)


def _norm_proj_kernel(x_ref, g_ref, w_ref, o_ref):
    hn = _rms(x_ref[...], g_ref[...]).astype(BF16)
    o_ref[0] = _dot(hn, w_ref[0]).astype(o_ref.dtype)


def _norm_proj(x2d, g, w_stack, tm):
    m, d = x2d.shape
    nl, _, n = w_stack.shape
    return pl.pallas_call(
        _norm_proj_kernel,
        grid=(nl, m // tm),
        in_specs=[
            pl.BlockSpec((tm, d), lambda l, i: (i, 0)),
            pl.BlockSpec((1, d), lambda l, i: (0, 0)),
            pl.BlockSpec((1, d, n), lambda l, i: (l, 0, 0)),
        ],
        out_specs=pl.BlockSpec((1, tm, n), lambda l, i: (l, i, 0)),
        out_shape=jax.ShapeDtypeStruct((nl, m, n), BF16),
        compiler_params=_params("arbitrary", "arbitrary"),
        name="mem_kv_proj",
    )(x2d, g.reshape(1, d), w_stack)


def _even_in_kernel(x_ref, g_ref, w_ref, cdft_ref, pq_ref, qkv_ref):
    fw = FOURIER_GROUPS * FOURIER_GROUP_DIM
    nw = NA_HEADS * NA_HEAD_DIM
    hn = _rms(x_ref[...], g_ref[...]).astype(BF16)
    z = _dot(hn, w_ref[...])
    q = z[:, fw:fw + nw] * (NA_HEAD_DIM ** -0.5)
    qkv_ref[:, :nw] = q.astype(BF16)
    qkv_ref[:, nw:] = z[:, fw + nw:].astype(BF16)
    cd = cdft_ref[...]
    gd = FOURIER_GROUP_DIM
    for grp in range(FOURIER_GROUPS):
        zf = z[:, grp * gd:(grp + 1) * gd].astype(BF16)
        pq = _dot(zf, cd)
        pq_ref[:, grp * gd:(grp + 1) * gd] = pq[:, :gd].astype(BF16)
        pq_ref[:, fw + grp * gd:fw + (grp + 1) * gd] = pq[:, gd:].astype(BF16)


def _even_in(h2d, g, w_in, cdft, tm):
    m, d = h2d.shape
    n = w_in.shape[1]
    fw = FOURIER_GROUPS * FOURIER_GROUP_DIM
    return pl.pallas_call(
        _even_in_kernel,
        grid=(m // tm,),
        in_specs=[
            pl.BlockSpec((tm, d), lambda i: (i, 0)),
            _const_spec((1, d)),
            _const_spec((d, n)),
            _const_spec(cdft.shape),
        ],
        out_specs=[
            pl.BlockSpec((tm, 2 * fw), lambda i: (i, 0)),
            pl.BlockSpec((tm, n - fw), lambda i: (i, 0)),
        ],
        out_shape=[
            jax.ShapeDtypeStruct((m, 2 * fw), BF16),
            jax.ShapeDtypeStruct((m, n - fw), BF16),
        ],
        compiler_params=_params("arbitrary"),
        name="even_in_proj",
    )(h2d, g.reshape(1, d), w_in, cdft)


def _seq_dft_kernel(m_ref, pq_ref, o_ref):
    s = pq_ref.shape[1]
    fw = o_ref.shape[2]
    y = _dot(m_ref[:, :s], pq_ref[0, :, :fw]) + _dot(m_ref[:, s:], pq_ref[0, :, fw:])
    o_ref[0] = y.astype(o_ref.dtype)


def _seq_dft(sdft, pq, tk):
    b, s, w2 = pq.shape
    fw = w2 // 2
    return pl.pallas_call(
        _seq_dft_kernel,
        grid=(b, s // tk),
        in_specs=[
            pl.BlockSpec((tk, 2 * s), lambda bi, k: (k, 0)),
            pl.BlockSpec((1, s, w2), lambda bi, k: (bi, 0, 0)),
        ],
        out_specs=pl.BlockSpec((1, tk, fw), lambda bi, k: (bi, k, 0)),
        out_shape=jax.ShapeDtypeStruct((b, s, fw), BF16),
        compiler_params=_params("arbitrary", "arbitrary"),
        name="seq_dft",
    )(sdft, pq)


NA_QROWS = 4
NA_KROWS = NA_QROWS + NA_KH


def _na_kernel(pat_ref, q_ref, k_ref, v_ref, bias_ref, o_ref, *, rows):
    del pat_ref
    i = pl.program_id(1)
    krow0 = jnp.clip(i * NA_QROWS - NA_KH // 2, 0, rows - NA_KROWS)
    kstart = pl.multiple_of(krow0 * GRID_W, GRID_W)
    nk = NA_KROWS * GRID_W
    dh = NA_HEAD_DIM
    scores = [_dot_nt(q_ref[0, :, hd * dh:(hd + 1) * dh],
                      k_ref[0, pl.ds(kstart, nk), hd * dh:(hd + 1) * dh]) for hd in range(NA_HEADS)]
    for hd in range(NA_HEADS):
        v = v_ref[0, pl.ds(kstart, nk), hd * dh:(hd + 1) * dh]
        s = scores[hd] + bias_ref[0, hd]
        o_ref[0, :, hd * dh:(hd + 1) * dh] = _softmax_pv(s, v).astype(o_ref.dtype)


def _na_bias_tables(rpb, rows):
    nblk = rows // NA_QROWS
    patterns, pat_of_block = [], []
    for blk in range(nblk):
        i0 = blk * NA_QROWS
        krow0 = min(max(i0 - NA_KH // 2, 0), rows - NA_KROWS)
        qi = np.arange(i0, i0 + NA_QROWS)
        rs = np.clip(qi - NA_KH // 2, 0, rows - NA_KH)
        key = (krow0 - i0, tuple((rs - i0).tolist()))
        if key not in patterns:
            patterns.append(key)
        pat_of_block.append(patterns.index(key))
    cols = np.arange(GRID_W)
    cs = np.clip(cols - NA_KW // 2, 0, GRID_W - NA_KW)
    col_valid = (cols[None, :] >= cs[:, None]) & (cols[None, :] < cs[:, None] + NA_KW)
    col_off = np.clip(cols[None, :] - cols[:, None] + NA_KW - 1, 0, 2 * NA_KW - 2)
    tables = []
    for dk, drs in patterns:
        a = np.arange(NA_QROWS)
        kr = dk + np.arange(NA_KROWS)
        rs_rel = np.asarray(drs)
        row_valid = (kr[None, :] >= rs_rel[:, None]) & (kr[None, :] < rs_rel[:, None] + NA_KH)
        row_off = np.clip(kr[None, :] - a[:, None] + NA_KH - 1, 0, 2 * NA_KH - 2)
        sel_r = (row_off[:, :, None] == np.arange(2 * NA_KH - 1)).astype(np.float32)
        sel_c = (col_off[:, :, None] == np.arange(2 * NA_KW - 1)).astype(np.float32)
        bias = jnp.einsum("hrk,amr,jck->hajmc", rpb, sel_r, sel_c, precision=lax.Precision.HIGHEST)
        valid = row_valid[:, None, :, None] & col_valid[None, :, None, :]
        bias = jnp.where(valid[None], bias, NEG_MASK)
        tables.append(bias.reshape(rpb.shape[0], NA_QROWS * GRID_W, NA_KROWS * GRID_W))
    return jnp.stack(tables).astype(F32), jnp.asarray(pat_of_block, jnp.int32)


def _neighborhood_attention(qkv, rpb):
    b, s, w3 = qkv.shape
    nw = w3 // 3
    rows = s // GRID_W
    assert rows >= NA_KROWS and rows % NA_QROWS == 0
    tables, pat = _na_bias_tables(rpb, rows)
    tq = NA_QROWS * GRID_W
    nk = NA_KROWS * GRID_W
    grid_spec = pltpu.PrefetchScalarGridSpec(
        num_scalar_prefetch=1,
        grid=(b, rows // NA_QROWS),
        in_specs=[
            pl.BlockSpec((1, tq, nw), lambda bi, i, p: (bi, i, 0)),
            pl.BlockSpec((1, s, nw), lambda bi, i, p: (bi, 0, 1)),
            pl.BlockSpec((1, s, nw), lambda bi, i, p: (bi, 0, 2)),
            pl.BlockSpec((1, NA_HEADS, tq, nk), lambda bi, i, p: (p[i], 0, 0, 0)),
        ],
        out_specs=pl.BlockSpec((1, tq, nw), lambda bi, i, p: (bi, i, 0)),
    )
    return pl.pallas_call(
        functools.partial(_na_kernel, rows=rows),
        grid_spec=grid_spec,
        out_shape=jax.ShapeDtypeStruct((b, s, nw), BF16),
        compiler_params=_params("arbitrary", "arbitrary"),
        name="neighborhood_attn",
    )(pat, qkv, qkv, qkv, tables)


def _even_out_kernel(h_ref, yf_ref, ya_ref, w_ref, o_ref):
    fw = yf_ref.shape[1]
    acc = _dot(yf_ref[...], w_ref[:fw, :]) + _dot(ya_ref[...], w_ref[fw:, :])
    o_ref[...] = h_ref[...] + acc


def _even_out(h2d, yf2d, ya2d, w_out, tm):
    m, d = h2d.shape
    fw, nw = yf2d.shape[1], ya2d.shape[1]
    return pl.pallas_call(
        _even_out_kernel,
        grid=(m // tm,),
        in_specs=[
            pl.BlockSpec((tm, d), lambda i: (i, 0)),
            pl.BlockSpec((tm, fw), lambda i: (i, 0)),
            pl.BlockSpec((tm, nw), lambda i: (i, 0)),
            _const_spec(w_out.shape),
        ],
        out_specs=pl.BlockSpec((tm, d), lambda i: (i, 0)),
        out_shape=jax.ShapeDtypeStruct((m, d), F32),
        compiler_params=_params("arbitrary"),
        name="even_out_proj",
    )(h2d, yf2d, ya2d, w_out)


def _halo_specs(tm, s, d):
    r = tm // HALO
    last_blk = s // HALO - 1

    def prev_map(b, i):
        return (b, jnp.maximum(i * r - 1, 0), 0)

    def next_map(b, i):
        return (b, jnp.minimum((i + 1) * r, last_blk), 0)

    return [
        pl.BlockSpec((1, tm, d), lambda b, i: (b, i, 0)),
        pl.BlockSpec((1, HALO, d), prev_map),
        pl.BlockSpec((1, HALO, d), next_map),
    ]


def _fill_normed_ext(hn_ref, x, xp, xn, g):
    tm = x.shape[0]
    i = pl.program_id(1)
    first = i == 0
    last = i == pl.num_programs(1) - 1
    hp = jnp.where(first, 0.0, _rms(xp, g))
    hx = jnp.where(last, 0.0, _rms(xn, g))
    hn_ref[:HALO, :] = hp.astype(BF16)
    hn_ref[HALO:HALO + tm, :] = _rms(x, g).astype(BF16)
    hn_ref[HALO + tm:, :] = hx.astype(BF16)


def _conv3_rows(z, cw, tm):
    n = z.shape[0]
    prev = pltpu.roll(z, 1, 0)[HALO:HALO + tm]
    nxt = pltpu.roll(z, n - 1, 0)[HALO:HALO + tm]
    return prev * cw[0:1, :] + z[HALO:HALO + tm] * cw[1:2, :] + nxt * cw[2:3, :]


def _odd_mixer_kernel(x_ref, xp_ref, xn_ref, g_ref, win_ref, cw_ref, wout_ref, o_ref, hn_ref, *, nc):
    tm, d = x_ref.shape[1], x_ref.shape[2]
    x = x_ref[0]
    _fill_normed_ext(hn_ref, x, xp_ref[0], xn_ref[0], g_ref[...])
    hn_ext = hn_ref[...]
    hn = hn_ref[HALO:HALO + tm, :]

    def project(c):
        lo, hi = c * nc, (c + 1) * nc
        return (_dot(hn, win_ref[:, lo:hi]),
                _dot(hn_ext, win_ref[:, d + lo:d + hi]),
                _dot(hn_ext, win_ref[:, 2 * d + lo:2 * d + hi]))

    nchunks = d // nc
    acc = x
    cur = project(0)
    for c in range(nchunks):
        nxt = project(c + 1) if c + 1 < nchunks else None
        gate_b, gate_c, u = cur
        lo, hi = c * nc, (c + 1) * nc
        y = gate_b * _conv3_rows(gate_c * u, cw_ref[:, lo:hi], tm)
        acc = acc + _dot(y.astype(BF16), wout_ref[lo:hi, :])
        cur = nxt
    o_ref[0] = acc


def _odd_mixer(h, g, w_in, conv_w, w_out, tm, nc=256):
    b, s, d = h.shape
    return pl.pallas_call(
        functools.partial(_odd_mixer_kernel, nc=nc),
        grid=(b, s // tm),
        in_specs=_halo_specs(tm, s, d) + [
            _const_spec((1, d)),
            _const_spec(w_in.shape),
            _const_spec(conv_w.shape),
            _const_spec(w_out.shape),
        ],
        out_specs=pl.BlockSpec((1, tm, d), lambda bi, i: (bi, i, 0)),
        out_shape=jax.ShapeDtypeStruct(h.shape, F32),
        scratch_shapes=[pltpu.VMEM((tm + 2 * HALO, d), BF16)],
        compiler_params=_params("arbitrary", "arbitrary"),
        name="odd_mixer",
    )(h, h, h, g.reshape(1, d), w_in, conv_w, w_out)


def _xattn_kernel(x_ref, g_ref, wq_ref, kv_ref, wo_ref, o_ref):
    d = x_ref.shape[2]
    dh = d // XA_HEADS
    x = x_ref[0]
    hn = _rms(x, g_ref[...]).astype(BF16)
    q = (_dot(hn, wq_ref[...]) * (dh ** -0.5)).astype(BF16)
    scores = [_dot_nt(q[:, hd * dh:(hd + 1) * dh], kv_ref[0, 0, :, hd * dh:(hd + 1) * dh])
              for hd in range(XA_HEADS)]
    acc = x
    for hd in range(XA_HEADS):
        v = kv_ref[0, 0, :, d + hd * dh:d + (hd + 1) * dh]
        oh = _softmax_pv(scores[hd], v).astype(BF16)
        acc = acc + _dot(oh, wo_ref[hd * dh:(hd + 1) * dh, :])
    o_ref[0] = acc


def _xattn(h, g, wq, kv, layer, wo, tm):
    b, s, d = h.shape
    n_mem = kv.shape[2]
    return pl.pallas_call(
        _xattn_kernel,
        grid=(b, s // tm),
        in_specs=[
            pl.BlockSpec((1, tm, d), lambda bi, i: (bi, i, 0)),
            _const_spec((1, d)),
            _const_spec(wq.shape),
            pl.BlockSpec((1, 1, n_mem, 2 * d), lambda bi, i: (layer, bi, 0, 0)),
            _const_spec(wo.shape),
        ],
        out_specs=pl.BlockSpec((1, tm, d), lambda bi, i: (bi, i, 0)),
        out_shape=jax.ShapeDtypeStruct(h.shape, F32),
        compiler_params=_params("arbitrary", "arbitrary"),
        name="mem_xattn",
    )(h, g.reshape(1, d), wq, kv, wo)


def _gelu_exact(x):
    return 0.5 * x * (1.0 + lax.erf(x * (1.0 / math.sqrt(2.0))))


def _ffn_kernel(x_ref, xp_ref, xn_ref, g_ref, wup_ref, cw_ref, cb_ref, wdn_ref, gf_ref, o_ref,
                hn_ref, *, nc, final_norm):
    tm = x_ref.shape[1]
    dff = wdn_ref.shape[0]
    x = x_ref[0]
    _fill_normed_ext(hn_ref, x, xp_ref[0], xn_ref[0], g_ref[...])
    hn_ext = hn_ref[...]
    hn = hn_ref[HALO:HALO + tm, :]

    def project(c):
        lo, hi = c * nc, (c + 1) * nc
        return _dot(hn, wup_ref[:, lo:hi]), _dot(hn_ext, wup_ref[:, dff + lo:dff + hi])

    nchunks = dff // nc
    acc = x
    cur = project(0)
    for c in range(nchunks):
        nxt = project(c + 1) if c + 1 < nchunks else None
        u, gp = cur
        lo, hi = c * nc, (c + 1) * nc
        gate = _conv3_rows(gp, cw_ref[:, lo:hi], tm) + cb_ref[:, lo:hi]
        a = _gelu_exact(gate) * u
        acc = acc + _dot(a.astype(BF16), wdn_ref[lo:hi, :])
        cur = nxt
    if final_norm:
        acc = _rms(acc, gf_ref[...])
    o_ref[0] = acc


def _conv_ffn(h, g, w_up, conv_w, conv_b, w_down, final_g, tm, nc=256):
    b, s, d = h.shape
    dff = w_down.shape[0]
    final_norm = final_g is not None
    gf = (final_g if final_norm else g).reshape(1, d)
    return pl.pallas_call(
        functools.partial(_ffn_kernel, nc=nc, final_norm=final_norm),
        grid=(b, s // tm),
        in_specs=_halo_specs(tm, s, d) + [
            _const_spec((1, d)),
            _const_spec(w_up.shape),
            _const_spec(conv_w.shape),
            _const_spec((1, dff)),
            _const_spec(w_down.shape),
            _const_spec((1, d)),
        ],
        out_specs=pl.BlockSpec((1, tm, d), lambda bi, i: (bi, i, 0)),
        out_shape=jax.ShapeDtypeStruct(h.shape, F32),
        scratch_shapes=[pltpu.VMEM((tm + 2 * HALO, d), BF16)],
        compiler_params=_params("arbitrary", "arbitrary"),
        name="conv_ffn",
    )(h, h, h, g.reshape(1, d), w_up, conv_w, conv_b.reshape(1, dff), w_down, gf)


def _dft_cos_sin(n):
    j = jnp.arange(n, dtype=jnp.int32)
    ang = ((j[:, None] * j[None, :]) % n).astype(F32) * (2.0 * math.pi / n)
    return jnp.cos(ang), jnp.sin(ang)


def _dft_constants(s):
    cc, sc = _dft_cos_sin(FOURIER_GROUP_DIM)
    cdft = (jnp.concatenate([cc, sc], axis=1) * (FOURIER_GROUP_DIM ** -0.5)).astype(BF16)
    n2 = GRID_W
    n1 = s // n2
    k = jnp.arange(s, dtype=jnp.int32)
    ang1 = ((n2 * jnp.arange(n1, dtype=jnp.int32)[:, None] * k[None, :]) % s).astype(F32) * (2.0 * math.pi / s)
    ang2 = ((jnp.arange(n2, dtype=jnp.int32)[:, None] * k[None, :]) % s).astype(F32) * (2.0 * math.pi / s)
    c1, s1 = jnp.cos(ang1)[:, None, :], jnp.sin(ang1)[:, None, :]
    c2, s2 = jnp.cos(ang2)[None, :, :], jnp.sin(ang2)[None, :, :]
    scale = s ** -0.5
    cs = ((c1 * c2 - s1 * s2) * scale).astype(BF16).reshape(s, s)
    ss = ((s1 * c2 + c1 * s2) * (-scale)).astype(BF16).reshape(s, s)
    return cdft, jnp.concatenate([cs, ss], axis=1)


def kernel(x, mem, mem_norm_g, mix_norm_g, w_in_ab, rpb, w_out_ab, w_in_c, conv_c, w_out_c,
           xa_norm_g, xa_wq, xa_wkv, xa_wo, ffn_norm_g, ffn_w_up, ffn_conv_w, ffn_conv_b,
           ffn_w_down, final_norm_g):
    b, s, d = x.shape
    depth = mix_norm_g.shape[0]
    n_mem = mem.shape[1]
    tm = min(512, s)
    bf = lambda w: w.astype(BF16)

    cdft, sdft = _dft_constants(s)
    kv = _norm_proj(mem.reshape(b * n_mem, d), mem_norm_g, bf(xa_wkv), tm=min(512, b * n_mem))
    kv = kv.reshape(depth, b, n_mem, 2 * d)

    h = x
    for layer in range(depth):
        j = layer // 2
        if layer % 2 == 0:
            pq, qkv = _even_in(h.reshape(b * s, d), mix_norm_g[layer], bf(w_in_ab[j]), cdft, tm)
            yf = _seq_dft(sdft, pq.reshape(b, s, -1), tk=tm)
            ya = _neighborhood_attention(qkv.reshape(b, s, -1), rpb[j])
            h = _even_out(h.reshape(b * s, d), yf.reshape(b * s, -1), ya.reshape(b * s, -1),
                          bf(w_out_ab[j]), tm).reshape(b, s, d)
        else:
            h = _odd_mixer(h, mix_norm_g[layer], bf(w_in_c[j]), conv_c[j], bf(w_out_c[j]), tm)
        h = _xattn(h, xa_norm_g[layer], bf(xa_wq[layer]), kv, layer, bf(xa_wo[layer]), tm)
        h = _conv_ffn(h, ffn_norm_g[layer], bf(ffn_w_up[layer]), ffn_conv_w[layer], ffn_conv_b[layer],
                      bf(ffn_w_down[layer]), final_norm_g if layer == depth - 1 else None, tm)
    return h
```

```python
import functools
import math

import numpy as np
import jax
import jax.numpy as jnp
from jax import lax
from jax.experimental import pallas as pl
from jax.experimental.pallas import tpu as pltpu

EPS = 1e-6
GRID_W = 64
NA_KH, NA_KW = 8, 16
NA_HEADS = 4
NA_HEAD_DIM = 128
FOURIER_GROUPS = 4
FOURIER_GROUP_DIM = 128
XA_HEADS = 4
NEG_MASK = -1e30

LANES = 128
SUBLANES = 8
HALO = 16
VMEM_LIMIT = 56 * 1024 * 1024

F32 = jnp.float32
BF16 = jnp.bfloat16

FW = FOURIER_GROUPS * FOURIER_GROUP_DIM
NW = NA_HEADS * NA_HEAD_DIM
NI = GRID_W


def _rms(x, g):
    ms = jnp.mean(x * x, axis=-1, keepdims=True)
    return (x * lax.rsqrt(ms + EPS)) * g


def _dot(a, b):
    return jnp.dot(a, b, preferred_element_type=F32)


def _dot_nt(a, b):
    return lax.dot_general(a, b, (((1,), (1,)), ((), ())), preferred_element_type=F32)


def _softmax_pv(s, v):
    m = jnp.max(s, axis=-1, keepdims=True)
    p = jnp.exp(s - m)
    l = jnp.sum(p, axis=-1, keepdims=True)
    return _dot(p.astype(BF16), v) / l


def _const_spec(shape):
    nd = len(shape)
    return pl.BlockSpec(shape, lambda *_: (0,) * nd, pipeline_mode=pl.Buffered(1))


def _params(*sem):
    return pltpu.CompilerParams(dimension_semantics=sem, vmem_limit_bytes=VMEM_LIMIT)


def _strided_pitch(n):
    p = -(-n // SUBLANES) * SUBLANES
    return p if (p // SUBLANES) % 2 == 1 else p + SUBLANES


def _norm_proj_kernel(x_ref, g_ref, w_ref, o_ref):
    hn = _rms(x_ref[...], g_ref[...]).astype(BF16)
    o_ref[0] = _dot(hn, w_ref[0]).astype(o_ref.dtype)


def _norm_proj(x2d, g, w_stack, tm):
    m, d = x2d.shape
    nl, _, n = w_stack.shape
    return pl.pallas_call(
        _norm_proj_kernel,
        grid=(nl, m // tm),
        in_specs=[
            pl.BlockSpec((tm, d), lambda l, i: (i, 0)),
            pl.BlockSpec((1, d), lambda l, i: (0, 0)),
            pl.BlockSpec((1, d, n), lambda l, i: (l, 0, 0)),
        ],
        out_specs=pl.BlockSpec((1, tm, n), lambda l, i: (l, i, 0)),
        out_shape=jax.ShapeDtypeStruct((nl, m, n), BF16),
        compiler_params=_params("arbitrary", "arbitrary"),
        name="mem_kv_proj",
    )(x2d, g.reshape(1, d), w_stack)


def _even_qkv_kernel(x_ref, g_ref, w_ref, qkv_ref):
    hn = _rms(x_ref[...], g_ref[...]).astype(BF16)
    z = _dot(hn, w_ref[...])
    qkv_ref[:, :NW] = (z[:, :NW] * (NA_HEAD_DIM ** -0.5)).astype(BF16)
    qkv_ref[:, NW:] = z[:, NW:].astype(BF16)


def _even_qkv(h2d, g, w_qkv, tm):
    m, d = h2d.shape
    n = w_qkv.shape[1]
    return pl.pallas_call(
        _even_qkv_kernel,
        grid=(m // tm,),
        in_specs=[pl.BlockSpec((tm, d), lambda i: (i, 0)), _const_spec((1, d)), _const_spec((d, n))],
        out_specs=pl.BlockSpec((tm, n), lambda i: (i, 0)),
        out_shape=jax.ShapeDtypeStruct((m, n), BF16),
        compiler_params=_params("arbitrary"),
        name="even_qkv_proj",
    )(h2d, g.reshape(1, d), w_qkv)


def _fourier_in_kernel(x_ref, g_ref, w_ref, cdft_ref, o_ref, sc_ref):
    no, d = x_ref.shape[1], x_ref.shape[3]
    gd = FOURIER_GROUP_DIM
    x = x_ref[0].reshape(no * SUBLANES, d)
    hn = _rms(x, g_ref[...]).astype(BF16)
    z = _dot(hn, w_ref[...])
    cd = cdft_ref[...]
    for grp in range(FOURIER_GROUPS):
        pq = _dot(z[:, grp * gd:(grp + 1) * gd].astype(BF16), cd)
        sc_ref[grp] = pq[:, :gd]
        sc_ref[FOURIER_GROUPS + grp] = pq[:, gd:]
    nslab = 2 * FOURIER_GROUPS
    for il in range(SUBLANES):
        rows = [sc_ref[sl, pl.ds(il, no, stride=SUBLANES), :] for sl in range(nslab)]
        o_ref[0, il] = jnp.concatenate(rows, axis=1).astype(BF16)


def _fourier_in(h, g, w_f, cdft):
    b, s, d = h.shape
    no = s // NI
    h4 = h.reshape(b, no, NI, d)
    return pl.pallas_call(
        _fourier_in_kernel,
        grid=(b, NI // SUBLANES),
        in_specs=[
            pl.BlockSpec((1, no, SUBLANES, d), lambda bi, i: (bi, 0, i, 0)),
            _const_spec((1, d)),
            _const_spec(w_f.shape),
            _const_spec(cdft.shape),
        ],
        out_specs=pl.BlockSpec((1, SUBLANES, no, 2 * FW), lambda bi, i: (bi, i, 0, 0)),
        out_shape=jax.ShapeDtypeStruct((b, NI, no, 2 * FW), BF16),
        scratch_shapes=[pltpu.VMEM((2 * FW // LANES, no * SUBLANES, LANES), F32)],
        compiler_params=_params("arbitrary", "arbitrary"),
        name="fourier_in_proj",
    )(h4, g.reshape(1, d), w_f, cdft)


def _fft_kernel(zt_ref, m1_ref, m3_ref, o_ref, bt_ref, yf_ref, *, pitch_b, pitch_y):
    no = zt_ref.shape[2]
    nslab = FW // LANES
    j = pl.program_id(1)

    @pl.when(j == 0)
    def _():
        def stage1(i, carry):
            zi = zt_ref[0, i]
            rhs = jnp.concatenate([zi[:, :FW], zi[:, FW:]], axis=0)
            a = _dot(m1_ref[i], rhs)
            row0 = pl.multiple_of(i * pitch_b, SUBLANES)
            for sl in range(nslab):
                bt_ref[sl, pl.ds(row0, 2 * no), :] = a[:, sl * LANES:(sl + 1) * LANES]
            return carry

        lax.fori_loop(0, NI, stage1, 0, unroll=32)

        def stage2(k1, carry):
            br = [bt_ref[sl, pl.ds(k1, NI, stride=pitch_b), :] for sl in range(nslab)]
            bi = [bt_ref[sl, pl.ds(no + k1, NI, stride=pitch_b), :] for sl in range(nslab)]
            rhs = jnp.concatenate([jnp.concatenate(br, axis=1).astype(BF16),
                                   jnp.concatenate(bi, axis=1).astype(BF16)], axis=0)
            y = _dot(m3_ref[...], rhs)
            for sl in range(nslab):
                yf_ref[sl, pl.ds(k1, NI, stride=pitch_y), :] = y[:, sl * LANES:(sl + 1) * LANES]
            return carry

        lax.fori_loop(0, no, stage2, 0, unroll=16)

    for k2l in range(SUBLANES):
        row0 = pl.multiple_of((j * SUBLANES + k2l) * pitch_y, SUBLANES)
        rows = jnp.concatenate([yf_ref[sl, pl.ds(row0, no), :] for sl in range(nslab)], axis=1)
        o_ref[0, k2l * no:(k2l + 1) * no, :] = rows.astype(BF16)


def _fft_mix(zt, m1, m3):
    b, _, no, _ = zt.shape
    s = NI * no
    tm = SUBLANES * no
    pitch_b = _strided_pitch(2 * no)
    pitch_y = _strided_pitch(no)
    return pl.pallas_call(
        functools.partial(_fft_kernel, pitch_b=pitch_b, pitch_y=pitch_y),
        grid=(b, NI // SUBLANES),
        in_specs=[
            pl.BlockSpec((1, NI, no, 2 * FW), lambda bi, j: (bi, 0, 0, 0)),
            _const_spec(m1.shape),
            _const_spec(m3.shape),
        ],
        out_specs=pl.BlockSpec((1, tm, FW), lambda bi, j: (bi, j, 0)),
        out_shape=jax.ShapeDtypeStruct((b, s, FW), BF16),
        scratch_shapes=[
            pltpu.VMEM((FW // LANES, NI * pitch_b, LANES), F32),
            pltpu.VMEM((FW // LANES, NI * pitch_y, LANES), F32),
        ],
        compiler_params=_params("arbitrary", "arbitrary"),
        name="fft_mix",
    )(zt, m1, m3)


def _dft_constants(s):
    n = FOURIER_GROUP_DIM
    jn = jnp.arange(n, dtype=jnp.int32)
    ang = ((jn[:, None] * jn[None, :]) % n).astype(F32) * (2.0 * math.pi / n)
    cdft = (jnp.concatenate([jnp.cos(ang), jnp.sin(ang)], axis=1) * (n ** -0.5)).astype(BF16)
    no = s // NI
    i = jnp.arange(NI, dtype=jnp.int32)[:, None, None]
    k1 = jnp.arange(no, dtype=jnp.int32)[None, :, None]
    o = jnp.arange(no, dtype=jnp.int32)[None, None, :]
    ang1 = ((k1 * (NI * o + i)) % s).astype(F32) * (2.0 * math.pi / s)
    c1, s1 = jnp.cos(ang1), jnp.sin(ang1)
    m1 = jnp.concatenate([jnp.concatenate([c1, -s1], axis=2),
                          jnp.concatenate([-s1, -c1], axis=2)], axis=1).astype(BF16)
    ji = jnp.arange(NI, dtype=jnp.int32)
    ang3 = ((ji[:, None] * ji[None, :]) % NI).astype(F32) * (2.0 * math.pi / NI)
    m3 = (jnp.concatenate([jnp.cos(ang3), jnp.sin(ang3)], axis=1) * (s ** -0.5)).astype(BF16)
    return cdft, m1, m3


NA_QROWS = 4
NA_KROWS = NA_QROWS + NA_KH


def _na_kernel(pat_ref, q_ref, k_ref, v_ref, bias_ref, yf_ref, x_ref, w_ref, o_ref, ya_ref, *, rows):
    del pat_ref
    i = pl.program_id(1)
    krow0 = jnp.clip(i * NA_QROWS - NA_KH // 2, 0, rows - NA_KROWS)
    kstart = pl.multiple_of(krow0 * GRID_W, GRID_W)
    nk = NA_KROWS * GRID_W
    dh = NA_HEAD_DIM
    for hd in range(NA_HEADS):
        q = q_ref[0, :, hd * dh:(hd + 1) * dh]
        k = k_ref[0, pl.ds(kstart, nk), hd * dh:(hd + 1) * dh]
        v = v_ref[0, pl.ds(kstart, nk), hd * dh:(hd + 1) * dh]
        s = _dot_nt(q, k) + bias_ref[0, hd]
        ya_ref[:, hd * dh:(hd + 1) * dh] = _softmax_pv(s, v).astype(BF16)
    o_ref[0] = x_ref[0] + _dot(yf_ref[0], w_ref[:FW, :]) + _dot(ya_ref[...], w_ref[FW:, :])


def _na_bias_tables(rpb, rows):
    nblk = rows // NA_QROWS
    patterns, pat_of_block = [], []
    for blk in range(nblk):
        i0 = blk * NA_QROWS
        krow0 = min(max(i0 - NA_KH // 2, 0), rows - NA_KROWS)
        qi = np.arange(i0, i0 + NA_QROWS)
        rs = np.clip(qi - NA_KH // 2, 0, rows - NA_KH)
        key = (krow0 - i0, tuple((rs - i0).tolist()))
        if key not in patterns:
            patterns.append(key)
        pat_of_block.append(patterns.index(key))
    cols = np.arange(GRID_W)
    cs = np.clip(cols - NA_KW // 2, 0, GRID_W - NA_KW)
    col_valid = (cols[None, :] >= cs[:, None]) & (cols[None, :] < cs[:, None] + NA_KW)
    col_off = np.clip(cols[None, :] - cols[:, None] + NA_KW - 1, 0, 2 * NA_KW - 2)
    tables = []
    for dk, drs in patterns:
        a = np.arange(NA_QROWS)
        kr = dk + np.arange(NA_KROWS)
        rs_rel = np.asarray(drs)
        row_valid = (kr[None, :] >= rs_rel[:, None]) & (kr[None, :] < rs_rel[:, None] + NA_KH)
        row_off = np.clip(kr[None, :] - a[:, None] + NA_KH - 1, 0, 2 * NA_KH - 2)
        sel_r = (row_off[:, :, None] == np.arange(2 * NA_KH - 1)).astype(np.float32)
        sel_c = (col_off[:, :, None] == np.arange(2 * NA_KW - 1)).astype(np.float32)
        bias = jnp.einsum("hrk,amr,jck->hajmc", rpb, sel_r, sel_c, precision=lax.Precision.HIGHEST)
        valid = row_valid[:, None, :, None] & col_valid[None, :, None, :]
        bias = jnp.where(valid[None], bias, NEG_MASK)
        tables.append(bias.reshape(rpb.shape[0], NA_QROWS * GRID_W, NA_KROWS * GRID_W))
    return jnp.stack(tables).astype(F32), jnp.asarray(pat_of_block, jnp.int32)


def _na_mix(qkv, rpb, yf, h, w_out):
    b, s, d = h.shape
    rows = s // GRID_W
    assert rows >= NA_KROWS and rows % NA_QROWS == 0
    tables, pat = _na_bias_tables(rpb, rows)
    tq = NA_QROWS * GRID_W
    nk = NA_KROWS * GRID_W
    grid_spec = pltpu.PrefetchScalarGridSpec(
        num_scalar_prefetch=1,
        grid=(b, rows // NA_QROWS),
        in_specs=[
            pl.BlockSpec((1, tq, NW), lambda bi, i, p: (bi, i, 0)),
            pl.BlockSpec((1, s, NW), lambda bi, i, p: (bi, 0, 1)),
            pl.BlockSpec((1, s, NW), lambda bi, i, p: (bi, 0, 2)),
            pl.BlockSpec((1, NA_HEADS, tq, nk), lambda bi, i, p: (p[i], 0, 0, 0)),
            pl.BlockSpec((1, tq, FW), lambda bi, i, p: (bi, i, 0)),
            pl.BlockSpec((1, tq, d), lambda bi, i, p: (bi, i, 0)),
            pl.BlockSpec(w_out.shape, lambda bi, i, p: (0, 0), pipeline_mode=pl.Buffered(1)),
        ],
        out_specs=pl.BlockSpec((1, tq, d), lambda bi, i, p: (bi, i, 0)),
        scratch_shapes=[pltpu.VMEM((tq, NW), BF16)],
    )
    return pl.pallas_call(
        functools.partial(_na_kernel, rows=rows),
        grid_spec=grid_spec,
        out_shape=jax.ShapeDtypeStruct(h.shape, F32),
        compiler_params=_params("arbitrary", "arbitrary"),
        name="neighborhood_attn",
    )(pat, qkv, qkv, qkv, tables, yf, h, w_out)


def _halo_specs(tm, s, d):
    r = tm // HALO
    last_blk = s // HALO - 1

    def prev_map(b, i):
        return (b, jnp.maximum(i * r - 1, 0), 0)

    def next_map(b, i):
        return (b, jnp.minimum((i + 1) * r, last_blk), 0)

    return [
        pl.BlockSpec((1, tm, d), lambda b, i: (b, i, 0)),
        pl.BlockSpec((1, HALO, d), prev_map),
        pl.BlockSpec((1, HALO, d), next_map),
    ]


def _fill_normed_ext(hn_ref, x, xp, xn, g):
    tm = x.shape[0]
    i = pl.program_id(1)
    first = i == 0
    last = i == pl.num_programs(1) - 1
    hp = jnp.where(first, 0.0, _rms(xp, g))
    hx = jnp.where(last, 0.0, _rms(xn, g))
    hn_ref[:HALO, :] = hp.astype(BF16)
    hn_ref[HALO:HALO + tm, :] = _rms(x, g).astype(BF16)
    hn_ref[HALO + tm:, :] = hx.astype(BF16)


def _conv3_rows(z, cw, tm):
    n = z.shape[0]
    prev = pltpu.roll(z, 1, 0)[HALO:HALO + tm]
    nxt = pltpu.roll(z, n - 1, 0)[HALO:HALO + tm]
    return prev * cw[0:1, :] + z[HALO:HALO + tm] * cw[1:2, :] + nxt * cw[2:3, :]


def _odd_mixer_kernel(x_ref, xp_ref, xn_ref, g_ref, win_ref, cw_ref, wout_ref, o_ref, hn_ref, *, nc):
    tm, d = x_ref.shape[1], x_ref.shape[2]
    x = x_ref[0]
    _fill_normed_ext(hn_ref, x, xp_ref[0], xn_ref[0], g_ref[...])
    hn_ext = hn_ref[...]
    hn = hn_ref[HALO:HALO + tm, :]

    def project(c):
        lo, hi = c * nc, (c + 1) * nc
        return (_dot(hn, win_ref[:, lo:hi]),
                _dot(hn_ext, win_ref[:, d + lo:d + hi]),
                _dot(hn_ext, win_ref[:, 2 * d + lo:2 * d + hi]))

    nchunks = d // nc
    acc = x
    cur = project(0)
    for c in range(nchunks):
        nxt = project(c + 1) if c + 1 < nchunks else None
        gate_b, gate_c, u = cur
        lo, hi = c * nc, (c + 1) * nc
        y = gate_b * _conv3_rows(gate_c * u, cw_ref[:, lo:hi], tm)
        acc = acc + _dot(y.astype(BF16), wout_ref[lo:hi, :])
        cur = nxt
    o_ref[0] = acc


def _odd_mixer(h, g, w_in, conv_w, w_out, tm, nc=256):
    b, s, d = h.shape
    return pl.pallas_call(
        functools.partial(_odd_mixer_kernel, nc=nc),
        grid=(b, s // tm),
        in_specs=_halo_specs(tm, s, d) + [
            _const_spec((1, d)),
            _const_spec(w_in.shape),
            _const_spec(conv_w.shape),
            _const_spec(w_out.shape),
        ],
        out_specs=pl.BlockSpec((1, tm, d), lambda bi, i: (bi, i, 0)),
        out_shape=jax.ShapeDtypeStruct(h.shape, F32),
        scratch_shapes=[pltpu.VMEM((tm + 2 * HALO, d), BF16)],
        compiler_params=_params("arbitrary", "arbitrary"),
        name="odd_mixer",
    )(h, h, h, g.reshape(1, d), w_in, conv_w, w_out)


def _xattn_kernel(x_ref, g_ref, wq_ref, kv_ref, wo_ref, o_ref, oh_ref):
    d = x_ref.shape[2]
    dh = d // XA_HEADS
    x = x_ref[0]
    hn = _rms(x, g_ref[...]).astype(BF16)
    q = (_dot(hn, wq_ref[...]) * (dh ** -0.5)).astype(BF16)
    for hd in range(XA_HEADS):
        k = kv_ref[0, 0, :, hd * dh:(hd + 1) * dh]
        v = kv_ref[0, 0, :, d + hd * dh:d + (hd + 1) * dh]
        s = _dot_nt(q[:, hd * dh:(hd + 1) * dh], k)
        oh_ref[:, hd * dh:(hd + 1) * dh] = _softmax_pv(s, v).astype(BF16)
    o_ref[0] = x + _dot(oh_ref[...], wo_ref[...])


def _xattn(h, g, wq, kv, layer, wo, tm):
    b, s, d = h.shape
    n_mem = kv.shape[2]
    return pl.pallas_call(
        _xattn_kernel,
        grid=(b, s // tm),
        in_specs=[
            pl.BlockSpec((1, tm, d), lambda bi, i: (bi, i, 0)),
            _const_spec((1, d)),
            _const_spec(wq.shape),
            pl.BlockSpec((1, 1, n_mem, 2 * d), lambda bi, i: (layer, bi, 0, 0)),
            _const_spec(wo.shape),
        ],
        out_specs=pl.BlockSpec((1, tm, d), lambda bi, i: (bi, i, 0)),
        out_shape=jax.ShapeDtypeStruct(h.shape, F32),
        scratch_shapes=[pltpu.VMEM((tm, d), BF16)],
        compiler_params=_params("arbitrary", "arbitrary"),
        name="mem_xattn",
    )(h, g.reshape(1, d), wq, kv, wo)


def _gelu_exact(x):
    return 0.5 * x * (1.0 + lax.erf(x * (1.0 / math.sqrt(2.0))))


def _ffn_kernel(x_ref, xp_ref, xn_ref, g_ref, wup_ref, cw_ref, cb_ref, wdn_ref, gf_ref, o_ref,
                hn_ref, *, nc, final_norm):
    tm = x_ref.shape[1]
    dff = wdn_ref.shape[0]
    x = x_ref[0]
    _fill_normed_ext(hn_ref, x, xp_ref[0], xn_ref[0], g_ref[...])
    hn_ext = hn_ref[...]
    hn = hn_ref[HALO:HALO + tm, :]

    def project(c):
        lo, hi = c * nc, (c + 1) * nc
        return _dot(hn, wup_ref[:, lo:hi]), _dot(hn_ext, wup_ref[:, dff + lo:dff + hi])

    nchunks = dff // nc
    acc = x
    cur = project(0)
    for c in range(nchunks):
        nxt = project(c + 1) if c + 1 < nchunks else None
        u, gp = cur
        lo, hi = c * nc, (c + 1) * nc
        gate = _conv3_rows(gp, cw_ref[:, lo:hi], tm) + cb_ref[:, lo:hi]
        a = _gelu_exact(gate) * u
        acc = acc + _dot(a.astype(BF16), wdn_ref[lo:hi, :])
        cur = nxt
    if final_norm:
        acc = _rms(acc, gf_ref[...])
    o_ref[0] = acc


def _conv_ffn(h, g, w_up, conv_w, conv_b, w_down, final_g, tm, nc=256):
    b, s, d = h.shape
    dff = w_down.shape[0]
    final_norm = final_g is not None
    gf = (final_g if final_norm else g).reshape(1, d)
    return pl.pallas_call(
        functools.partial(_ffn_kernel, nc=nc, final_norm=final_norm),
        grid=(b, s // tm),
        in_specs=_halo_specs(tm, s, d) + [
            _const_spec((1, d)),
            _const_spec(w_up.shape),
            _const_spec(conv_w.shape),
            _const_spec((1, dff)),
            _const_spec(w_down.shape),
            _const_spec((1, d)),
        ],
        out_specs=pl.BlockSpec((1, tm, d), lambda bi, i: (bi, i, 0)),
        out_shape=jax.ShapeDtypeStruct(h.shape, F32),
        scratch_shapes=[pltpu.VMEM((tm + 2 * HALO, d), BF16)],
        compiler_params=_params("arbitrary", "arbitrary"),
        name="conv_ffn",
    )(h, h, h, g.reshape(1, d), w_up, conv_w, conv_b.reshape(1, dff), w_down, gf)


def kernel(x, mem, mem_norm_g, mix_norm_g, w_in_ab, rpb, w_out_ab, w_in_c, conv_c, w_out_c,
           xa_norm_g, xa_wq, xa_wkv, xa_wo, ffn_norm_g, ffn_w_up, ffn_conv_w, ffn_conv_b,
           ffn_w_down, final_norm_g):
    b, s, d = x.shape
    depth = mix_norm_g.shape[0]
    n_mem = mem.shape[1]
    tm = min(512, s)
    bf = lambda w: w.astype(BF16)

    cdft, m1, m3 = _dft_constants(s)
    kv = _norm_proj(mem.reshape(b * n_mem, d), mem_norm_g, bf(xa_wkv), tm=min(512, b * n_mem))
    kv = kv.reshape(depth, b, n_mem, 2 * d)

    h = x
    for layer in range(depth):
        j = layer // 2
        if layer % 2 == 0:
            g = mix_norm_g[layer]
            zt = _fourier_in(h, g, bf(w_in_ab[j, :, :FW]), cdft)
            qkv = _even_qkv(h.reshape(b * s, d), g, bf(w_in_ab[j, :, FW:]), tm).reshape(b, s, -1)
            yf = _fft_mix(zt, m1, m3)
            h = _na_mix(qkv, rpb[j], yf, h, bf(w_out_ab[j]))
        else:
            h = _odd_mixer(h, mix_norm_g[layer], bf(w_in_c[j]), conv_c[j], bf(w_out_c[j]), tm)
        h = _xattn(h, xa_norm_g[layer], bf(xa_wq[layer]), kv, layer, bf(xa_wo[layer]), tm)
        h = _conv_ffn(h, ffn_norm_g[layer], bf(ffn_w_up[layer]), ffn_conv_w[layer], ffn_conv_b[layer],
                      bf(ffn_w_down[layer]), final_norm_g if layer == depth - 1 else None, tm)
    return h
```

```python
import functools
import math

import numpy as np
import jax
import jax.numpy as jnp
from jax import lax
from jax.experimental import pallas as pl
from jax.experimental.pallas import tpu as pltpu

EPS = 1e-6
GRID_W = 64
NA_KH, NA_KW = 8, 16
NA_HEADS = 4
NA_HEAD_DIM = 128
FOURIER_GROUPS = 4
FOURIER_GROUP_DIM = 128
XA_HEADS = 4
NEG_MASK = -1e30

LANES = 128
SUBLANES = 8
HALO = 16
VMEM_LIMIT = 56 * 1024 * 1024

F32 = jnp.float32
BF16 = jnp.bfloat16

FW = FOURIER_GROUPS * FOURIER_GROUP_DIM
NW = NA_HEADS * NA_HEAD_DIM
NI = GRID_W


def _rms(x, g):
    ms = jnp.mean(x * x, axis=-1, keepdims=True)
    return (x * lax.rsqrt(ms + EPS)) * g


def _dot(a, b):
    return jnp.dot(a, b, preferred_element_type=F32)


def _dot_nt(a, b):
    return lax.dot_general(a, b, (((1,), (1,)), ((), ())), preferred_element_type=F32)


LOG2E = 1.4426950408889634


def _softmax_pv(s, v):
    m = jnp.max(s, axis=-1, keepdims=True)
    p = jnp.exp2(s - m)
    l = jnp.sum(p, axis=-1, keepdims=True)
    return _dot(p.astype(BF16), v) / l


def _const_spec(shape):
    nd = len(shape)
    return pl.BlockSpec(shape, lambda *_: (0,) * nd, pipeline_mode=pl.Buffered(1))


def _params(*sem):
    return pltpu.CompilerParams(dimension_semantics=sem, vmem_limit_bytes=VMEM_LIMIT)


def _strided_pitch(n):
    p = -(-n // SUBLANES) * SUBLANES
    return p if (p // SUBLANES) % 2 == 1 else p + SUBLANES


def _norm_proj_kernel(x_ref, g_ref, w_ref, o_ref):
    hn = _rms(x_ref[...], g_ref[...]).astype(BF16)
    o_ref[0] = _dot(hn, w_ref[0]).astype(o_ref.dtype)


def _norm_proj(x2d, g, w_stack, tm):
    m, d = x2d.shape
    nl, _, n = w_stack.shape
    return pl.pallas_call(
        _norm_proj_kernel,
        grid=(nl, m // tm),
        in_specs=[
            pl.BlockSpec((tm, d), lambda l, i: (i, 0)),
            pl.BlockSpec((1, d), lambda l, i: (0, 0)),
            pl.BlockSpec((1, d, n), lambda l, i: (l, 0, 0)),
        ],
        out_specs=pl.BlockSpec((1, tm, n), lambda l, i: (l, i, 0)),
        out_shape=jax.ShapeDtypeStruct((nl, m, n), BF16),
        compiler_params=_params("arbitrary", "arbitrary"),
        name="mem_kv_proj",
    )(x2d, g.reshape(1, d), w_stack)


def _even_qkv_kernel(x_ref, g_ref, w_ref, qkv_ref):
    hn = _rms(x_ref[...], g_ref[...]).astype(BF16)
    z = _dot(hn, w_ref[...])
    qkv_ref[:, :NW] = (z[:, :NW] * (LOG2E * NA_HEAD_DIM ** -0.5)).astype(BF16)
    qkv_ref[:, NW:] = z[:, NW:].astype(BF16)


def _even_qkv(h2d, g, w_qkv, tm):
    m, d = h2d.shape
    n = w_qkv.shape[1]
    return pl.pallas_call(
        _even_qkv_kernel,
        grid=(m // tm,),
        in_specs=[pl.BlockSpec((tm, d), lambda i: (i, 0)), _const_spec((1, d)), _const_spec((d, n))],
        out_specs=pl.BlockSpec((tm, n), lambda i: (i, 0)),
        out_shape=jax.ShapeDtypeStruct((m, n), BF16),
        compiler_params=_params("arbitrary"),
        name="even_qkv_proj",
    )(h2d, g.reshape(1, d), w_qkv)


def _fourier_in_kernel(x_ref, g_ref, w_ref, cdft_ref, o_ref, sc_ref):
    no, d = x_ref.shape[1], x_ref.shape[3]
    gd = FOURIER_GROUP_DIM
    x = x_ref[0].reshape(no * SUBLANES, d)
    hn = _rms(x, g_ref[...]).astype(BF16)
    z = _dot(hn, w_ref[...])
    cd = cdft_ref[...]
    for grp in range(FOURIER_GROUPS):
        pq = _dot(z[:, grp * gd:(grp + 1) * gd].astype(BF16), cd)
        sc_ref[grp] = pq[:, :gd]
        sc_ref[FOURIER_GROUPS + grp] = pq[:, gd:]
    nslab = 2 * FOURIER_GROUPS
    for il in range(SUBLANES):
        rows = [sc_ref[sl, pl.ds(il, no, stride=SUBLANES), :] for sl in range(nslab)]
        o_ref[0, il] = jnp.concatenate(rows, axis=1).astype(BF16)


def _fourier_in(h, g, w_f, cdft):
    b, s, d = h.shape
    no = s // NI
    h4 = h.reshape(b, no, NI, d)
    return pl.pallas_call(
        _fourier_in_kernel,
        grid=(b, NI // SUBLANES),
        in_specs=[
            pl.BlockSpec((1, no, SUBLANES, d), lambda bi, i: (bi, 0, i, 0)),
            _const_spec((1, d)),
            _const_spec(w_f.shape),
            _const_spec(cdft.shape),
        ],
        out_specs=pl.BlockSpec((1, SUBLANES, no, 2 * FW), lambda bi, i: (bi, i, 0, 0)),
        out_shape=jax.ShapeDtypeStruct((b, NI, no, 2 * FW), BF16),
        scratch_shapes=[pltpu.VMEM((2 * FW // LANES, no * SUBLANES, LANES), F32)],
        compiler_params=_params("arbitrary", "arbitrary"),
        name="fourier_in_proj",
    )(h4, g.reshape(1, d), w_f, cdft)


def _fft_kernel(zt_ref, m1_ref, m3_ref, o_ref, bt_ref, yf_ref, *, pitch_b, pitch_y):
    no = zt_ref.shape[2]
    nslab = FW // LANES
    j = pl.program_id(1)

    @pl.when(j == 0)
    def _():
        def stage1(i, carry):
            zi = zt_ref[0, i]
            rhs = jnp.concatenate([zi[:, :FW], zi[:, FW:]], axis=0)
            a = _dot(m1_ref[i], rhs)
            row0 = pl.multiple_of(i * pitch_b, SUBLANES)
            for sl in range(nslab):
                bt_ref[sl, pl.ds(row0, 2 * no), :] = a[:, sl * LANES:(sl + 1) * LANES]
            return carry

        lax.fori_loop(0, NI, stage1, 0, unroll=32)

        def stage2(k1, carry):
            br = [bt_ref[sl, pl.ds(k1, NI, stride=pitch_b), :] for sl in range(nslab)]
            bi = [bt_ref[sl, pl.ds(no + k1, NI, stride=pitch_b), :] for sl in range(nslab)]
            rhs = jnp.concatenate([jnp.concatenate(br, axis=1).astype(BF16),
                                   jnp.concatenate(bi, axis=1).astype(BF16)], axis=0)
            y = _dot(m3_ref[...], rhs)
            for sl in range(nslab):
                yf_ref[sl, pl.ds(k1, NI, stride=pitch_y), :] = y[:, sl * LANES:(sl + 1) * LANES]
            return carry

        lax.fori_loop(0, no, stage2, 0, unroll=16)

    for k2l in range(SUBLANES):
        row0 = pl.multiple_of((j * SUBLANES + k2l) * pitch_y, SUBLANES)
        rows = jnp.concatenate([yf_ref[sl, pl.ds(row0, no), :] for sl in range(nslab)], axis=1)
        o_ref[0, k2l * no:(k2l + 1) * no, :] = rows.astype(BF16)


def _fft_mix(zt, m1, m3):
    b, _, no, _ = zt.shape
    s = NI * no
    tm = SUBLANES * no
    pitch_b = _strided_pitch(2 * no)
    pitch_y = _strided_pitch(no)
    return pl.pallas_call(
        functools.partial(_fft_kernel, pitch_b=pitch_b, pitch_y=pitch_y),
        grid=(b, NI // SUBLANES),
        in_specs=[
            pl.BlockSpec((1, NI, no, 2 * FW), lambda bi, j: (bi, 0, 0, 0)),
            _const_spec(m1.shape),
            _const_spec(m3.shape),
        ],
        out_specs=pl.BlockSpec((1, tm, FW), lambda bi, j: (bi, j, 0)),
        out_shape=jax.ShapeDtypeStruct((b, s, FW), BF16),
        scratch_shapes=[
            pltpu.VMEM((FW // LANES, NI * pitch_b, LANES), F32),
            pltpu.VMEM((FW // LANES, NI * pitch_y, LANES), F32),
        ],
        compiler_params=_params("arbitrary", "arbitrary"),
        name="fft_mix",
    )(zt, m1, m3)


def _dft_constants(s):
    n = FOURIER_GROUP_DIM
    jn = jnp.arange(n, dtype=jnp.int32)
    ang = ((jn[:, None] * jn[None, :]) % n).astype(F32) * (2.0 * math.pi / n)
    cdft = (jnp.concatenate([jnp.cos(ang), jnp.sin(ang)], axis=1) * (n ** -0.5)).astype(BF16)
    no = s // NI
    i = jnp.arange(NI, dtype=jnp.int32)[:, None, None]
    k1 = jnp.arange(no, dtype=jnp.int32)[None, :, None]
    o = jnp.arange(no, dtype=jnp.int32)[None, None, :]
    ang1 = ((k1 * (NI * o + i)) % s).astype(F32) * (2.0 * math.pi / s)
    c1, s1 = jnp.cos(ang1), jnp.sin(ang1)
    m1 = jnp.concatenate([jnp.concatenate([c1, -s1], axis=2),
                          jnp.concatenate([-s1, -c1], axis=2)], axis=1).astype(BF16)
    ji = jnp.arange(NI, dtype=jnp.int32)
    ang3 = ((ji[:, None] * ji[None, :]) % NI).astype(F32) * (2.0 * math.pi / NI)
    m3 = (jnp.concatenate([jnp.cos(ang3), jnp.sin(ang3)], axis=1) * (s ** -0.5)).astype(BF16)
    return cdft, m1, m3


NA_QROWS = 4
NA_KROWS = NA_QROWS + NA_KH


def _na_kernel(pat_ref, q_ref, k_ref, v_ref, bias_ref, yf_ref, x_ref, w_ref, o_ref, ya_ref, s_ref, *, rows):
    del pat_ref
    i = pl.program_id(1)
    krow0 = jnp.clip(i * NA_QROWS - NA_KH // 2, 0, rows - NA_KROWS)
    kstart = pl.multiple_of(krow0 * GRID_W, GRID_W)
    nk = NA_KROWS * GRID_W
    dh = NA_HEAD_DIM
    for hd in range(NA_HEADS):
        q = q_ref[0, :, hd * dh:(hd + 1) * dh]
        k = k_ref[0, pl.ds(kstart, nk), hd * dh:(hd + 1) * dh]
        s_ref[hd] = _dot_nt(q, k) + bias_ref[0, hd]
    for hd in range(NA_HEADS):
        v = v_ref[0, pl.ds(kstart, nk), hd * dh:(hd + 1) * dh]
        ya_ref[:, hd * dh:(hd + 1) * dh] = _softmax_pv(s_ref[hd], v).astype(BF16)
    o_ref[0] = x_ref[0] + _dot(yf_ref[0], w_ref[:FW, :]) + _dot(ya_ref[...], w_ref[FW:, :])


def _na_bias_tables(rpb, rows):
    nblk = rows // NA_QROWS
    patterns, pat_of_block = [], []
    for blk in range(nblk):
        i0 = blk * NA_QROWS
        krow0 = min(max(i0 - NA_KH // 2, 0), rows - NA_KROWS)
        qi = np.arange(i0, i0 + NA_QROWS)
        rs = np.clip(qi - NA_KH // 2, 0, rows - NA_KH)
        key = (krow0 - i0, tuple((rs - i0).tolist()))
        if key not in patterns:
            patterns.append(key)
        pat_of_block.append(patterns.index(key))
    cols = np.arange(GRID_W)
    cs = np.clip(cols - NA_KW // 2, 0, GRID_W - NA_KW)
    col_valid = (cols[None, :] >= cs[:, None]) & (cols[None, :] < cs[:, None] + NA_KW)
    col_off = np.clip(cols[None, :] - cols[:, None] + NA_KW - 1, 0, 2 * NA_KW - 2)
    tables = []
    for dk, drs in patterns:
        a = np.arange(NA_QROWS)
        kr = dk + np.arange(NA_KROWS)
        rs_rel = np.asarray(drs)
        row_valid = (kr[None, :] >= rs_rel[:, None]) & (kr[None, :] < rs_rel[:, None] + NA_KH)
        row_off = np.clip(kr[None, :] - a[:, None] + NA_KH - 1, 0, 2 * NA_KH - 2)
        sel_r = (row_off[:, :, None] == np.arange(2 * NA_KH - 1)).astype(np.float32)
        sel_c = (col_off[:, :, None] == np.arange(2 * NA_KW - 1)).astype(np.float32)
        bias = jnp.einsum("hrk,amr,jck->hajmc", rpb, sel_r, sel_c, precision=lax.Precision.HIGHEST)
        valid = row_valid[:, None, :, None] & col_valid[None, :, None, :]
        bias = jnp.where(valid[None], bias * LOG2E, NEG_MASK)
        tables.append(bias.reshape(rpb.shape[0], NA_QROWS * GRID_W, NA_KROWS * GRID_W))
    return jnp.stack(tables).astype(F32), jnp.asarray(pat_of_block, jnp.int32)


def _na_mix(qkv, rpb, yf, h, w_out):
    b, s, d = h.shape
    rows = s // GRID_W
    assert rows >= NA_KROWS and rows % NA_QROWS == 0
    tables, pat = _na_bias_tables(rpb, rows)
    tq = NA_QROWS * GRID_W
    nk = NA_KROWS * GRID_W
    grid_spec = pltpu.PrefetchScalarGridSpec(
        num_scalar_prefetch=1,
        grid=(b, rows // NA_QROWS),
        in_specs=[
            pl.BlockSpec((1, tq, NW), lambda bi, i, p: (bi, i, 0)),
            pl.BlockSpec((1, s, NW), lambda bi, i, p: (bi, 0, 1)),
            pl.BlockSpec((1, s, NW), lambda bi, i, p: (bi, 0, 2)),
            pl.BlockSpec((1, NA_HEADS, tq, nk), lambda bi, i, p: (p[i], 0, 0, 0)),
            pl.BlockSpec((1, tq, FW), lambda bi, i, p: (bi, i, 0)),
            pl.BlockSpec((1, tq, d), lambda bi, i, p: (bi, i, 0)),
            pl.BlockSpec(w_out.shape, lambda bi, i, p: (0, 0), pipeline_mode=pl.Buffered(1)),
        ],
        out_specs=pl.BlockSpec((1, tq, d), lambda bi, i, p: (bi, i, 0)),
        scratch_shapes=[pltpu.VMEM((tq, NW), BF16), pltpu.VMEM((NA_HEADS, tq, nk), F32)],
    )
    return pl.pallas_call(
        functools.partial(_na_kernel, rows=rows),
        grid_spec=grid_spec,
        out_shape=jax.ShapeDtypeStruct(h.shape, F32),
        compiler_params=_params("arbitrary", "arbitrary"),
        name="neighborhood_attn",
    )(pat, qkv, qkv, qkv, tables, yf, h, w_out)


def _halo_specs(tm, s, d):
    r = tm // HALO
    last_blk = s // HALO - 1

    def prev_map(b, i):
        return (b, jnp.maximum(i * r - 1, 0), 0)

    def next_map(b, i):
        return (b, jnp.minimum((i + 1) * r, last_blk), 0)

    return [
        pl.BlockSpec((1, tm, d), lambda b, i: (b, i, 0)),
        pl.BlockSpec((1, HALO, d), prev_map),
        pl.BlockSpec((1, HALO, d), next_map),
    ]


def _fill_normed_ext(hn_ref, x, xp, xn, g):
    tm = x.shape[0]
    i = pl.program_id(1)
    first = i == 0
    last = i == pl.num_programs(1) - 1
    hp = jnp.where(first, 0.0, _rms(xp, g))
    hx = jnp.where(last, 0.0, _rms(xn, g))
    hn_ref[:HALO, :] = hp.astype(BF16)
    hn_ref[HALO:HALO + tm, :] = _rms(x, g).astype(BF16)
    hn_ref[HALO + tm:, :] = hx.astype(BF16)


def _conv3_rows(z, cw, tm):
    n = z.shape[0]
    prev = pltpu.roll(z, 1, 0)[HALO:HALO + tm]
    nxt = pltpu.roll(z, n - 1, 0)[HALO:HALO + tm]
    return prev * cw[0:1, :] + z[HALO:HALO + tm] * cw[1:2, :] + nxt * cw[2:3, :]


def _odd_mixer_kernel(x_ref, xp_ref, xn_ref, g_ref, win_ref, cw_ref, wout_ref, o_ref, hn_ref, *, nc):
    tm, d = x_ref.shape[1], x_ref.shape[2]
    x = x_ref[0]
    _fill_normed_ext(hn_ref, x, xp_ref[0], xn_ref[0], g_ref[...])
    hn_ext = hn_ref[...]
    hn = hn_ref[HALO:HALO + tm, :]

    def project(c):
        lo, hi = c * nc, (c + 1) * nc
        return (_dot(hn, win_ref[:, lo:hi]),
                _dot(hn_ext, win_ref[:, d + lo:d + hi]),
                _dot(hn_ext, win_ref[:, 2 * d + lo:2 * d + hi]))

    nchunks = d // nc
    acc = x
    cur = project(0)
    for c in range(nchunks):
        nxt = project(c + 1) if c + 1 < nchunks else None
        gate_b, gate_c, u = cur
        lo, hi = c * nc, (c + 1) * nc
        y = gate_b * _conv3_rows(gate_c * u, cw_ref[:, lo:hi], tm)
        acc = acc + _dot(y.astype(BF16), wout_ref[lo:hi, :])
        cur = nxt
    o_ref[0] = acc


def _odd_mixer(h, g, w_in, conv_w, w_out, tm, nc=256):
    b, s, d = h.shape
    return pl.pallas_call(
        functools.partial(_odd_mixer_kernel, nc=nc),
        grid=(b, s // tm),
        in_specs=_halo_specs(tm, s, d) + [
            _const_spec((1, d)),
            _const_spec(w_in.shape),
            _const_spec(conv_w.shape),
            _const_spec(w_out.shape),
        ],
        out_specs=pl.BlockSpec((1, tm, d), lambda bi, i: (bi, i, 0)),
        out_shape=jax.ShapeDtypeStruct(h.shape, F32),
        scratch_shapes=[pltpu.VMEM((tm + 2 * HALO, d), BF16)],
        compiler_params=_params("arbitrary", "arbitrary"),
        name="odd_mixer",
    )(h, h, h, g.reshape(1, d), w_in, conv_w, w_out)


def _xattn_kernel(x_ref, g_ref, wq_ref, kv_ref, wo_ref, o_ref, oh_ref, s_ref):
    d = x_ref.shape[2]
    dh = d // XA_HEADS
    x = x_ref[0]
    hn = _rms(x, g_ref[...]).astype(BF16)
    q = (_dot(hn, wq_ref[...]) * (LOG2E * dh ** -0.5)).astype(BF16)
    for hd in range(XA_HEADS):
        k = kv_ref[0, 0, :, hd * dh:(hd + 1) * dh]
        s_ref[hd] = _dot_nt(q[:, hd * dh:(hd + 1) * dh], k)
    for hd in range(XA_HEADS):
        v = kv_ref[0, 0, :, d + hd * dh:d + (hd + 1) * dh]
        oh_ref[:, hd * dh:(hd + 1) * dh] = _softmax_pv(s_ref[hd], v).astype(BF16)
    o_ref[0] = x + _dot(oh_ref[...], wo_ref[...])


def _xattn(h, g, wq, kv, layer, wo, tm):
    b, s, d = h.shape
    n_mem = kv.shape[2]
    return pl.pallas_call(
        _xattn_kernel,
        grid=(b, s // tm),
        in_specs=[
            pl.BlockSpec((1, tm, d), lambda bi, i: (bi, i, 0)),
            _const_spec((1, d)),
            _const_spec(wq.shape),
            pl.BlockSpec((1, 1, n_mem, 2 * d), lambda bi, i: (layer, bi, 0, 0)),
            _const_spec(wo.shape),
        ],
        out_specs=pl.BlockSpec((1, tm, d), lambda bi, i: (bi, i, 0)),
        out_shape=jax.ShapeDtypeStruct(h.shape, F32),
        scratch_shapes=[pltpu.VMEM((tm, d), BF16), pltpu.VMEM((XA_HEADS, tm, n_mem), F32)],
        compiler_params=_params("arbitrary", "arbitrary"),
        name="mem_xattn",
    )(h, g.reshape(1, d), wq, kv, wo)


def _gelu_exact(x):
    return 0.5 * x * (1.0 + lax.erf(x * (1.0 / math.sqrt(2.0))))


def _ffn_kernel(x_ref, xp_ref, xn_ref, g_ref, wup_ref, cw_ref, cb_ref, wdn_ref, gf_ref, o_ref,
                hn_ref, *, nc, final_norm):
    tm = x_ref.shape[1]
    dff = wdn_ref.shape[0]
    x = x_ref[0]
    _fill_normed_ext(hn_ref, x, xp_ref[0], xn_ref[0], g_ref[...])
    hn_ext = hn_ref[...]
    hn = hn_ref[HALO:HALO + tm, :]

    def project(c):
        lo, hi = c * nc, (c + 1) * nc
        return _dot(hn, wup_ref[:, lo:hi]), _dot(hn_ext, wup_ref[:, dff + lo:dff + hi])

    nchunks = dff // nc
    acc = x
    cur = project(0)
    for c in range(nchunks):
        nxt = project(c + 1) if c + 1 < nchunks else None
        u, gp = cur
        lo, hi = c * nc, (c + 1) * nc
        gate = _conv3_rows(gp, cw_ref[:, lo:hi], tm) + cb_ref[:, lo:hi]
        a = _gelu_exact(gate) * u
        acc = acc + _dot(a.astype(BF16), wdn_ref[lo:hi, :])
        cur = nxt
    if final_norm:
        acc = _rms(acc, gf_ref[...])
    o_ref[0] = acc


def _conv_ffn(h, g, w_up, conv_w, conv_b, w_down, final_g, tm, nc=256):
    b, s, d = h.shape
    dff = w_down.shape[0]
    final_norm = final_g is not None
    gf = (final_g if final_norm else g).reshape(1, d)
    return pl.pallas_call(
        functools.partial(_ffn_kernel, nc=nc, final_norm=final_norm),
        grid=(b, s // tm),
        in_specs=_halo_specs(tm, s, d) + [
            _const_spec((1, d)),
            _const_spec(w_up.shape),
            _const_spec(conv_w.shape),
            _const_spec((1, dff)),
            _const_spec(w_down.shape),
            _const_spec((1, d)),
        ],
        out_specs=pl.BlockSpec((1, tm, d), lambda bi, i: (bi, i, 0)),
        out_shape=jax.ShapeDtypeStruct(h.shape, F32),
        scratch_shapes=[pltpu.VMEM((tm + 2 * HALO, d), BF16)],
        compiler_params=_params("arbitrary", "arbitrary"),
        name="conv_ffn",
    )(h, h, h, g.reshape(1, d), w_up, conv_w, conv_b.reshape(1, dff), w_down, gf)


def kernel(x, mem, mem_norm_g, mix_norm_g, w_in_ab, rpb, w_out_ab, w_in_c, conv_c, w_out_c,
           xa_norm_g, xa_wq, xa_wkv, xa_wo, ffn_norm_g, ffn_w_up, ffn_conv_w, ffn_conv_b,
           ffn_w_down, final_norm_g):
    b, s, d = x.shape
    depth = mix_norm_g.shape[0]
    n_mem = mem.shape[1]
    tm = min(512, s)
    bf = lambda w: w.astype(BF16)

    cdft, m1, m3 = _dft_constants(s)
    kv = _norm_proj(mem.reshape(b * n_mem, d), mem_norm_g, bf(xa_wkv), tm=min(512, b * n_mem))
    kv = kv.reshape(depth, b, n_mem, 2 * d)

    h = x
    for layer in range(depth):
        j = layer // 2
        if layer % 2 == 0:
            g = mix_norm_g[layer]
            zt = _fourier_in(h, g, bf(w_in_ab[j, :, :FW]), cdft)
            qkv = _even_qkv(h.reshape(b * s, d), g, bf(w_in_ab[j, :, FW:]), tm).reshape(b, s, -1)
            yf = _fft_mix(zt, m1, m3)
            h = _na_mix(qkv, rpb[j], yf, h, bf(w_out_ab[j]))
        else:
            h = _odd_mixer(h, mix_norm_g[layer], bf(w_in_c[j]), conv_c[j], bf(w_out_c[j]), tm)
        h = _xattn(h, xa_norm_g[layer], bf(xa_wq[layer]), kv, layer, bf(xa_wo[layer]), tm)
        h = _conv_ffn(h, ffn_norm_g[layer], bf(ffn_w_up[layer]), ffn_conv_w[layer], ffn_conv_b[layer],
                      bf(ffn_w_down[layer]), final_norm_g if layer == depth - 1 else None, tm)
    return h
```

```python
import functools
import math

import numpy as np
import jax
import jax.numpy as jnp
from jax import lax
from jax.experimental import pallas as pl
from jax.experimental.pallas import tpu as pltpu

EPS = 1e-6
GRID_W = 64
NA_KH, NA_KW = 8, 16
NA_HEADS = 4
NA_HEAD_DIM = 128
FOURIER_GROUPS = 4
FOURIER_GROUP_DIM = 128
XA_HEADS = 4
NEG_MASK = -1e30

LANES = 128
SUBLANES = 8
HALO = 16
VMEM_LIMIT = 56 * 1024 * 1024

F32 = jnp.float32
BF16 = jnp.bfloat16

FW = FOURIER_GROUPS * FOURIER_GROUP_DIM
NW = NA_HEADS * NA_HEAD_DIM
NI = GRID_W


def _rms(x, g):
    ms = jnp.mean(x * x, axis=-1, keepdims=True)
    return (x * lax.rsqrt(ms + EPS)) * g


def _dot(a, b):
    return jnp.dot(a, b, preferred_element_type=F32)


def _dot_nt(a, b):
    return lax.dot_general(a, b, (((1,), (1,)), ((), ())), preferred_element_type=F32)


LOG2E = 1.4426950408889634


def _softmax_pv(s, v):
    m = jnp.max(s, axis=-1, keepdims=True)
    p = jnp.exp2(s - m)
    l = jnp.sum(p, axis=-1, keepdims=True)
    return _dot(p.astype(BF16), v) / l


def _const_spec(shape):
    nd = len(shape)
    return pl.BlockSpec(shape, lambda *_: (0,) * nd, pipeline_mode=pl.Buffered(1))


def _params(*sem):
    return pltpu.CompilerParams(dimension_semantics=sem, vmem_limit_bytes=VMEM_LIMIT)


def _strided_pitch(n):
    p = -(-n // SUBLANES) * SUBLANES
    return p if (p // SUBLANES) % 2 == 1 else p + SUBLANES


def _mem_fold_kernel(m_ref, g_ref, wkv_ref, wq_ref, wo_ref, mq_ref, no_ref):
    n_mem, d = m_ref.shape
    dh = d // XA_HEADS
    mn = _rms(m_ref[...], g_ref[...]).astype(BF16)
    kv = _dot(mn, wkv_ref[0]).astype(BF16)
    for hd in range(XA_HEADS):
        k = kv[:, hd * dh:(hd + 1) * dh]
        v = kv[:, d + hd * dh:d + (hd + 1) * dh]
        mq = _dot_nt(wq_ref[0, :, hd * dh:(hd + 1) * dh], k) * (LOG2E * dh ** -0.5)
        mq_ref[0, 0, :, hd * n_mem:(hd + 1) * n_mem] = mq.astype(BF16)
        no_ref[0, 0, hd * n_mem:(hd + 1) * n_mem, :] = _dot(v, wo_ref[0, hd * dh:(hd + 1) * dh, :]).astype(BF16)


def _mem_fold(mem2d, g, wkv, wq, wo, b):
    bm, d = mem2d.shape
    n_mem = bm // b
    nl = wkv.shape[0]
    hm = XA_HEADS * n_mem
    return pl.pallas_call(
        _mem_fold_kernel,
        grid=(nl, b),
        in_specs=[
            pl.BlockSpec((n_mem, d), lambda l, bi: (bi, 0)),
            pl.BlockSpec((1, d), lambda l, bi: (0, 0)),
            pl.BlockSpec((1, d, 2 * d), lambda l, bi: (l, 0, 0)),
            pl.BlockSpec((1, d, d), lambda l, bi: (l, 0, 0)),
            pl.BlockSpec((1, d, d), lambda l, bi: (l, 0, 0)),
        ],
        out_specs=[pl.BlockSpec((1, 1, d, hm), lambda l, bi: (l, bi, 0, 0)),
                   pl.BlockSpec((1, 1, hm, d), lambda l, bi: (l, bi, 0, 0))],
        out_shape=[jax.ShapeDtypeStruct((nl, b, d, hm), BF16), jax.ShapeDtypeStruct((nl, b, hm, d), BF16)],
        compiler_params=_params("arbitrary", "arbitrary"),
        name="mem_fold",
    )(mem2d, g.reshape(1, d), wkv, wq, wo)


def _even_qkv_kernel(x_ref, g_ref, w_ref, qv_ref, kt_ref):
    hn = _rms(x_ref[0], g_ref[...]).astype(BF16)
    z = _dot(hn, w_ref[...])
    qv_ref[0, :, :NW] = (z[:, :NW] * (LOG2E * NA_HEAD_DIM ** -0.5)).astype(BF16)
    qv_ref[0, :, NW:] = z[:, 2 * NW:].astype(BF16)
    kt_ref[0] = z[:, NW:2 * NW].T.astype(BF16)


def _even_qkv(h, g, w_qkv, tm):
    b, s, d = h.shape
    n = w_qkv.shape[1]
    return pl.pallas_call(
        _even_qkv_kernel,
        grid=(b, s // tm),
        in_specs=[pl.BlockSpec((1, tm, d), lambda bi, i: (bi, i, 0)), _const_spec((1, d)), _const_spec((d, n))],
        out_specs=[pl.BlockSpec((1, tm, 2 * NW), lambda bi, i: (bi, i, 0)),
                   pl.BlockSpec((1, NW, tm), lambda bi, i: (bi, 0, i))],
        out_shape=[jax.ShapeDtypeStruct((b, s, 2 * NW), BF16), jax.ShapeDtypeStruct((b, NW, s), BF16)],
        compiler_params=_params("arbitrary", "arbitrary"),
        name="even_qkv_proj",
    )(h, g.reshape(1, d), w_qkv)


def _fourier_in_kernel(x_ref, g_ref, w_ref, cdft_ref, o_ref, sc_ref):
    no, d = x_ref.shape[1], x_ref.shape[3]
    gd = FOURIER_GROUP_DIM
    x = x_ref[0].reshape(no * SUBLANES, d)
    hn = _rms(x, g_ref[...]).astype(BF16)
    z = _dot(hn, w_ref[...])
    cd = cdft_ref[...]
    for grp in range(FOURIER_GROUPS):
        pq = _dot(z[:, grp * gd:(grp + 1) * gd].astype(BF16), cd)
        sc_ref[grp] = pq[:, :gd]
        sc_ref[FOURIER_GROUPS + grp] = pq[:, gd:]
    nslab = 2 * FOURIER_GROUPS
    for il in range(SUBLANES):
        rows = [sc_ref[sl, pl.ds(il, no, stride=SUBLANES), :] for sl in range(nslab)]
        o_ref[0, il] = jnp.concatenate(rows, axis=1).astype(BF16)


def _fourier_in(h, g, w_f, cdft):
    b, s, d = h.shape
    no = s // NI
    h4 = h.reshape(b, no, NI, d)
    return pl.pallas_call(
        _fourier_in_kernel,
        grid=(b, NI // SUBLANES),
        in_specs=[
            pl.BlockSpec((1, no, SUBLANES, d), lambda bi, i: (bi, 0, i, 0)),
            _const_spec((1, d)),
            _const_spec(w_f.shape),
            _const_spec(cdft.shape),
        ],
        out_specs=pl.BlockSpec((1, SUBLANES, no, 2 * FW), lambda bi, i: (bi, i, 0, 0)),
        out_shape=jax.ShapeDtypeStruct((b, NI, no, 2 * FW), BF16),
        scratch_shapes=[pltpu.VMEM((2 * FW // LANES, no * SUBLANES, LANES), F32)],
        compiler_params=_params("arbitrary", "arbitrary"),
        name="fourier_in_proj",
    )(h4, g.reshape(1, d), w_f, cdft)


def _fft_kernel(zt_ref, m1_ref, m3_ref, o_ref, bt_ref, yf_ref, *, pitch_b, pitch_y):
    no = zt_ref.shape[2]
    nslab = FW // LANES
    j = pl.program_id(1)

    @pl.when(j == 0)
    def _():
        def stage1(i, carry):
            zi = zt_ref[0, i]
            rhs = jnp.concatenate([zi[:, :FW], zi[:, FW:]], axis=0)
            a = _dot(m1_ref[i], rhs)
            row0 = pl.multiple_of(i * pitch_b, SUBLANES)
            for sl in range(nslab):
                bt_ref[sl, pl.ds(row0, 2 * no), :] = a[:, sl * LANES:(sl + 1) * LANES]
            return carry

        lax.fori_loop(0, NI, stage1, 0, unroll=32)

        def stage2(k1, carry):
            br = [bt_ref[sl, pl.ds(k1, NI, stride=pitch_b), :] for sl in range(nslab)]
            bi = [bt_ref[sl, pl.ds(no + k1, NI, stride=pitch_b), :] for sl in range(nslab)]
            rhs = jnp.concatenate([jnp.concatenate(br, axis=1).astype(BF16),
                                   jnp.concatenate(bi, axis=1).astype(BF16)], axis=0)
            y = _dot(m3_ref[...], rhs)
            for sl in range(nslab):
                yf_ref[sl, pl.ds(k1, NI, stride=pitch_y), :] = y[:, sl * LANES:(sl + 1) * LANES]
            return carry

        lax.fori_loop(0, no, stage2, 0, unroll=16)

    for k2l in range(SUBLANES):
        row0 = pl.multiple_of((j * SUBLANES + k2l) * pitch_y, SUBLANES)
        rows = jnp.concatenate([yf_ref[sl, pl.ds(row0, no), :] for sl in range(nslab)], axis=1)
        o_ref[0, k2l * no:(k2l + 1) * no, :] = rows.astype(BF16)


def _fft_mix(zt, m1, m3):
    b, _, no, _ = zt.shape
    s = NI * no
    tm = SUBLANES * no
    pitch_b = _strided_pitch(2 * no)
    pitch_y = _strided_pitch(no)
    return pl.pallas_call(
        functools.partial(_fft_kernel, pitch_b=pitch_b, pitch_y=pitch_y),
        grid=(b, NI // SUBLANES),
        in_specs=[
            pl.BlockSpec((1, NI, no, 2 * FW), lambda bi, j: (jnp.minimum(bi + (j > 0), b - 1), 0, 0, 0)),
            _const_spec(m1.shape),
            _const_spec(m3.shape),
        ],
        out_specs=pl.BlockSpec((1, tm, FW), lambda bi, j: (bi, j, 0)),
        out_shape=jax.ShapeDtypeStruct((b, s, FW), BF16),
        scratch_shapes=[
            pltpu.VMEM((FW // LANES, NI * pitch_b, LANES), F32),
            pltpu.VMEM((FW // LANES, NI * pitch_y, LANES), F32),
        ],
        compiler_params=_params("arbitrary", "arbitrary"),
        name="fft_mix",
    )(zt, m1, m3)


def _dft_constants(s):
    n = FOURIER_GROUP_DIM
    jn = jnp.arange(n, dtype=jnp.int32)
    ang = ((jn[:, None] * jn[None, :]) % n).astype(F32) * (2.0 * math.pi / n)
    cdft = (jnp.concatenate([jnp.cos(ang), jnp.sin(ang)], axis=1) * (n ** -0.5)).astype(BF16)
    no = s // NI
    i = jnp.arange(NI, dtype=jnp.int32)[:, None, None]
    k1 = jnp.arange(no, dtype=jnp.int32)[None, :, None]
    o = jnp.arange(no, dtype=jnp.int32)[None, None, :]
    ang1 = ((k1 * (NI * o + i)) % s).astype(F32) * (2.0 * math.pi / s)
    c1, s1 = jnp.cos(ang1), jnp.sin(ang1)
    m1 = jnp.concatenate([jnp.concatenate([c1, -s1], axis=2),
                          jnp.concatenate([-s1, -c1], axis=2)], axis=1).astype(BF16)
    ji = jnp.arange(NI, dtype=jnp.int32)
    ang3 = ((ji[:, None] * ji[None, :]) % NI).astype(F32) * (2.0 * math.pi / NI)
    m3 = (jnp.concatenate([jnp.cos(ang3), jnp.sin(ang3)], axis=1) * (s ** -0.5)).astype(BF16)
    return cdft, m1, m3


NA_QROWS = 4
NA_KROWS = NA_QROWS + NA_KH


def _na_kernel(pat_ref, q_ref, kt_ref, v_ref, bias_ref, yf_ref, x_ref, w_ref, o_ref, ya_ref, s_ref, *, rows):
    del pat_ref
    i = pl.program_id(1)
    krow0 = jnp.clip(i * NA_QROWS - NA_KH // 2, 0, rows - NA_KROWS)
    kstart = pl.multiple_of(krow0 * GRID_W, NA_QROWS * GRID_W)
    nk = NA_KROWS * GRID_W
    dh = NA_HEAD_DIM
    for hd in range(NA_HEADS):
        q = q_ref[0, :, hd * dh:(hd + 1) * dh]
        kt = kt_ref[0, hd * dh:(hd + 1) * dh, pl.ds(kstart, nk)]
        s_ref[hd] = _dot(q, kt) + bias_ref[0, hd]
    for hd in range(NA_HEADS):
        v = v_ref[0, pl.ds(kstart, nk), hd * dh:(hd + 1) * dh]
        ya_ref[:, hd * dh:(hd + 1) * dh] = _softmax_pv(s_ref[hd], v).astype(BF16)
    o_ref[0] = x_ref[0] + _dot(yf_ref[0], w_ref[:FW, :]) + _dot(ya_ref[...], w_ref[FW:, :])


def _na_bias_tables(rpb, rows):
    nblk = rows // NA_QROWS
    patterns, pat_of_block = [], []
    for blk in range(nblk):
        i0 = blk * NA_QROWS
        krow0 = min(max(i0 - NA_KH // 2, 0), rows - NA_KROWS)
        qi = np.arange(i0, i0 + NA_QROWS)
        rs = np.clip(qi - NA_KH // 2, 0, rows - NA_KH)
        key = (krow0 - i0, tuple((rs - i0).tolist()))
        if key not in patterns:
            patterns.append(key)
        pat_of_block.append(patterns.index(key))
    cols = np.arange(GRID_W)
    cs = np.clip(cols - NA_KW // 2, 0, GRID_W - NA_KW)
    col_valid = (cols[None, :] >= cs[:, None]) & (cols[None, :] < cs[:, None] + NA_KW)
    col_off = np.clip(cols[None, :] - cols[:, None] + NA_KW - 1, 0, 2 * NA_KW - 2)
    tables = []
    for dk, drs in patterns:
        a = np.arange(NA_QROWS)
        kr = dk + np.arange(NA_KROWS)
        rs_rel = np.asarray(drs)
        row_valid = (kr[None, :] >= rs_rel[:, None]) & (kr[None, :] < rs_rel[:, None] + NA_KH)
        row_off = np.clip(kr[None, :] - a[:, None] + NA_KH - 1, 0, 2 * NA_KH - 2)
        sel_r = (row_off[:, :, None] == np.arange(2 * NA_KH - 1)).astype(np.float32)
        sel_c = (col_off[:, :, None] == np.arange(2 * NA_KW - 1)).astype(np.float32)
        bias = jnp.einsum("hrk,amr,jck->hajmc", rpb, sel_r, sel_c, precision=lax.Precision.HIGHEST)
        valid = row_valid[:, None, :, None] & col_valid[None, :, None, :]
        bias = jnp.where(valid[None], bias * LOG2E, NEG_MASK)
        tables.append(bias.reshape(rpb.shape[0], NA_QROWS * GRID_W, NA_KROWS * GRID_W))
    return jnp.stack(tables).astype(F32), jnp.asarray(pat_of_block, jnp.int32)


def _na_mix(qv, kt, rpb, yf, h, w_out):
    b, s, d = h.shape
    rows = s // GRID_W
    assert rows >= NA_KROWS and rows % NA_QROWS == 0
    tables, pat = _na_bias_tables(rpb, rows)
    tq = NA_QROWS * GRID_W
    nk = NA_KROWS * GRID_W
    grid_spec = pltpu.PrefetchScalarGridSpec(
        num_scalar_prefetch=1,
        grid=(b, rows // NA_QROWS),
        in_specs=[
            pl.BlockSpec((1, tq, NW), lambda bi, i, p: (bi, i, 0)),
            pl.BlockSpec((1, NW, s), lambda bi, i, p: (bi, 0, 0)),
            pl.BlockSpec((1, s, NW), lambda bi, i, p: (bi, 0, 1)),
            pl.BlockSpec((1, NA_HEADS, tq, nk), lambda bi, i, p: (p[i], 0, 0, 0)),
            pl.BlockSpec((1, tq, FW), lambda bi, i, p: (bi, i, 0)),
            pl.BlockSpec((1, tq, d), lambda bi, i, p: (bi, i, 0)),
            pl.BlockSpec(w_out.shape, lambda bi, i, p: (0, 0), pipeline_mode=pl.Buffered(1)),
        ],
        out_specs=pl.BlockSpec((1, tq, d), lambda bi, i, p: (bi, i, 0)),
        scratch_shapes=[pltpu.VMEM((tq, NW), BF16), pltpu.VMEM((NA_HEADS, tq, nk), F32)],
    )
    return pl.pallas_call(
        functools.partial(_na_kernel, rows=rows),
        grid_spec=grid_spec,
        out_shape=jax.ShapeDtypeStruct(h.shape, F32),
        compiler_params=_params("arbitrary", "arbitrary"),
        name="neighborhood_attn",
    )(pat, qv, kt, qv, tables, yf, h, w_out)


def _halo_specs(tm, s, d):
    r = tm // HALO
    last_blk = s // HALO - 1

    def prev_map(b, i):
        return (b, jnp.maximum(i * r - 1, 0), 0)

    def next_map(b, i):
        return (b, jnp.minimum((i + 1) * r, last_blk), 0)

    return [
        pl.BlockSpec((1, tm, d), lambda b, i: (b, i, 0)),
        pl.BlockSpec((1, HALO, d), prev_map),
        pl.BlockSpec((1, HALO, d), next_map),
    ]


def _fill_normed_ext(hn_ref, x, xp, xn, g):
    tm = x.shape[0]
    i = pl.program_id(1)
    first = i == 0
    last = i == pl.num_programs(1) - 1
    hp = jnp.where(first, 0.0, _rms(xp, g))
    hx = jnp.where(last, 0.0, _rms(xn, g))
    hn_ref[:HALO, :] = hp.astype(BF16)
    hn_ref[HALO:HALO + tm, :] = _rms(x, g).astype(BF16)
    hn_ref[HALO + tm:, :] = hx.astype(BF16)


def _conv3_rows(z, cw, tm):
    n = z.shape[0]
    prev = pltpu.roll(z, 1, 0)[HALO:HALO + tm]
    nxt = pltpu.roll(z, n - 1, 0)[HALO:HALO + tm]
    return prev * cw[0:1, :] + z[HALO:HALO + tm] * cw[1:2, :] + nxt * cw[2:3, :]


def _odd_mixer_kernel(x_ref, xp_ref, xn_ref, g_ref, win_ref, cw_ref, wout_ref, o_ref, hn_ref, *, nc):
    tm, d = x_ref.shape[1], x_ref.shape[2]
    x = x_ref[0]
    _fill_normed_ext(hn_ref, x, xp_ref[0], xn_ref[0], g_ref[...])
    hn_ext = hn_ref[...]
    hn = hn_ref[HALO:HALO + tm, :]

    def project(c):
        lo, hi = c * nc, (c + 1) * nc
        return (_dot(hn, win_ref[:, lo:hi]),
                _dot(hn_ext, win_ref[:, d + lo:d + hi]),
                _dot(hn_ext, win_ref[:, 2 * d + lo:2 * d + hi]))

    nchunks = d // nc
    acc = x
    cur = project(0)
    for c in range(nchunks):
        nxt = project(c + 1) if c + 1 < nchunks else None
        gate_b, gate_c, u = cur
        lo, hi = c * nc, (c + 1) * nc
        y = gate_b * _conv3_rows(gate_c * u, cw_ref[:, lo:hi], tm)
        acc = acc + _dot(y.astype(BF16), wout_ref[lo:hi, :])
        cur = nxt
    o_ref[0] = acc


def _odd_mixer(h, g, w_in, conv_w, w_out, tm, nc=256):
    b, s, d = h.shape
    return pl.pallas_call(
        functools.partial(_odd_mixer_kernel, nc=nc),
        grid=(b, s // tm),
        in_specs=_halo_specs(tm, s, d) + [
            _const_spec((1, d)),
            _const_spec(w_in.shape),
            _const_spec(conv_w.shape),
            _const_spec(w_out.shape),
        ],
        out_specs=pl.BlockSpec((1, tm, d), lambda bi, i: (bi, i, 0)),
        out_shape=jax.ShapeDtypeStruct(h.shape, F32),
        scratch_shapes=[pltpu.VMEM((tm + 2 * HALO, d), BF16)],
        compiler_params=_params("arbitrary", "arbitrary"),
        name="odd_mixer",
    )(h, h, h, g.reshape(1, d), w_in, conv_w, w_out)


def _xattn_kernel(x_ref, g_ref, mq_ref, no_ref, o_ref, p_ref, s_ref):
    n_mem = mq_ref.shape[3] // XA_HEADS
    x = x_ref[0]
    hn = _rms(x, g_ref[...]).astype(BF16)
    s_ref[...] = _dot(hn, mq_ref[0, 0])
    for hd in range(XA_HEADS):
        s = s_ref[:, hd * n_mem:(hd + 1) * n_mem]
        p = jnp.exp2(s - jnp.max(s, axis=-1, keepdims=True))
        r = 1.0 / jnp.sum(p, axis=-1, keepdims=True)
        p_ref[:, hd * n_mem:(hd + 1) * n_mem] = (p * r).astype(BF16)
    o_ref[0] = x + _dot(p_ref[...], no_ref[0, 0])


def _xattn(h, g, mq, no, layer, tm):
    b, s, d = h.shape
    hm = mq.shape[3]
    return pl.pallas_call(
        _xattn_kernel,
        grid=(b, s // tm),
        in_specs=[
            pl.BlockSpec((1, tm, d), lambda bi, i: (bi, i, 0)),
            _const_spec((1, d)),
            pl.BlockSpec((1, 1, d, hm), lambda bi, i: (layer, bi, 0, 0)),
            pl.BlockSpec((1, 1, hm, d), lambda bi, i: (layer, bi, 0, 0)),
        ],
        out_specs=pl.BlockSpec((1, tm, d), lambda bi, i: (bi, i, 0)),
        out_shape=jax.ShapeDtypeStruct(h.shape, F32),
        scratch_shapes=[pltpu.VMEM((tm, hm), BF16), pltpu.VMEM((tm, hm), F32)],
        compiler_params=_params("arbitrary", "arbitrary"),
        name="mem_xattn",
    )(h, g.reshape(1, d), mq, no)


def _gelu_exact(x):
    return 0.5 * x * (1.0 + lax.erf(x * (1.0 / math.sqrt(2.0))))


def _ffn_kernel(x_ref, xp_ref, xn_ref, g_ref, wup_ref, cw_ref, cb_ref, wdn_ref, gf_ref, o_ref,
                hn_ref, *, nc, final_norm):
    tm = x_ref.shape[1]
    dff = wdn_ref.shape[0]
    x = x_ref[0]
    _fill_normed_ext(hn_ref, x, xp_ref[0], xn_ref[0], g_ref[...])
    hn_ext = hn_ref[...]
    hn = hn_ref[HALO:HALO + tm, :]

    def project(c):
        lo, hi = c * nc, (c + 1) * nc
        return _dot(hn, wup_ref[:, lo:hi]), _dot(hn_ext, wup_ref[:, dff + lo:dff + hi])

    nchunks = dff // nc
    acc = x
    cur = project(0)
    for c in range(nchunks):
        nxt = project(c + 1) if c + 1 < nchunks else None
        u, gp = cur
        lo, hi = c * nc, (c + 1) * nc
        gate = _conv3_rows(gp, cw_ref[:, lo:hi], tm) + cb_ref[:, lo:hi]
        a = _gelu_exact(gate) * u
        acc = acc + _dot(a.astype(BF16), wdn_ref[lo:hi, :])
        cur = nxt
    if final_norm:
        acc = _rms(acc, gf_ref[...])
    o_ref[0] = acc


def _conv_ffn(h, g, w_up, conv_w, conv_b, w_down, final_g, tm, nc=256):
    b, s, d = h.shape
    dff = w_down.shape[0]
    final_norm = final_g is not None
    gf = (final_g if final_norm else g).reshape(1, d)
    return pl.pallas_call(
        functools.partial(_ffn_kernel, nc=nc, final_norm=final_norm),
        grid=(b, s // tm),
        in_specs=_halo_specs(tm, s, d) + [
            _const_spec((1, d)),
            _const_spec(w_up.shape),
            _const_spec(conv_w.shape),
            _const_spec((1, dff)),
            _const_spec(w_down.shape),
            _const_spec((1, d)),
        ],
        out_specs=pl.BlockSpec((1, tm, d), lambda bi, i: (bi, i, 0)),
        out_shape=jax.ShapeDtypeStruct(h.shape, F32),
        scratch_shapes=[pltpu.VMEM((tm + 2 * HALO, d), BF16)],
        compiler_params=_params("arbitrary", "arbitrary"),
        name="conv_ffn",
    )(h, h, h, g.reshape(1, d), w_up, conv_w, conv_b.reshape(1, dff), w_down, gf)


def kernel(x, mem, mem_norm_g, mix_norm_g, w_in_ab, rpb, w_out_ab, w_in_c, conv_c, w_out_c,
           xa_norm_g, xa_wq, xa_wkv, xa_wo, ffn_norm_g, ffn_w_up, ffn_conv_w, ffn_conv_b,
           ffn_w_down, final_norm_g):
    b, s, d = x.shape
    depth = mix_norm_g.shape[0]
    n_mem = mem.shape[1]
    tm = min(512, s)
    bf = lambda w: w.astype(BF16)

    cdft, m1, m3 = _dft_constants(s)
    mq, no = _mem_fold(mem.reshape(b * n_mem, d), mem_norm_g, bf(xa_wkv), bf(xa_wq), bf(xa_wo), b)

    h = x
    for layer in range(depth):
        j = layer // 2
        if layer % 2 == 0:
            g = mix_norm_g[layer]
            zt = _fourier_in(h, g, bf(w_in_ab[j, :, :FW]), cdft)
            qv, kt = _even_qkv(h, g, bf(w_in_ab[j, :, FW:]), tm)
            yf = _fft_mix(zt, m1, m3)
            h = _na_mix(qv, kt, rpb[j], yf, h, bf(w_out_ab[j]))
        else:
            h = _odd_mixer(h, mix_norm_g[layer], bf(w_in_c[j]), conv_c[j], bf(w_out_c[j]), tm)
        h = _xattn(h, xa_norm_g[layer], mq, no, layer, tm)
        h = _conv_ffn(h, ffn_norm_g[layer], bf(ffn_w_up[layer]), ffn_conv_w[layer], ffn_conv_b[layer],
                      bf(ffn_w_down[layer]), final_norm_g if layer == depth - 1 else None, tm)
    return h
```

```python
import functools
import math

import numpy as np
import jax
import jax.numpy as jnp
from jax import lax
from jax.experimental import pallas as pl
from jax.experimental.pallas import tpu as pltpu

EPS = 1e-6
GRID_W = 64
NA_KH, NA_KW = 8, 16
NA_HEADS = 4
NA_HEAD_DIM = 128
FOURIER_GROUPS = 4
FOURIER_GROUP_DIM = 128
XA_HEADS = 4
NEG_MASK = -1e30

LANES = 128
SUBLANES = 8
HALO = 16
SUB_TILE = 512
TOKEN_TILE = 1024
VMEM_LIMIT = 56 * 1024 * 1024

F32 = jnp.float32
BF16 = jnp.bfloat16

FW = FOURIER_GROUPS * FOURIER_GROUP_DIM
NW = NA_HEADS * NA_HEAD_DIM
NI = GRID_W


def _rms(x, g):
    ms = jnp.mean(x * x, axis=-1, keepdims=True)
    return (x * lax.rsqrt(ms + EPS)) * g


def _dot(a, b):
    return jnp.dot(a, b, preferred_element_type=F32)


def _dot_nt(a, b):
    return lax.dot_general(a, b, (((1,), (1,)), ((), ())), preferred_element_type=F32)


LOG2E = 1.4426950408889634


def _softmax_pv(s, v):
    m = jnp.max(s, axis=-1, keepdims=True)
    p = jnp.exp2(s - m)
    l = jnp.sum(p, axis=-1, keepdims=True)
    return _dot(p.astype(BF16), v) / l


def _const_spec(shape):
    nd = len(shape)
    return pl.BlockSpec(shape, lambda *_: (0,) * nd, pipeline_mode=pl.Buffered(1))


def _params(*sem):
    return pltpu.CompilerParams(dimension_semantics=sem, vmem_limit_bytes=VMEM_LIMIT)


def _strided_pitch(n):
    p = -(-n // SUBLANES) * SUBLANES
    return p if (p // SUBLANES) % 2 == 1 else p + SUBLANES


def _mem_fold_kernel(m_ref, g_ref, wkv_ref, wq_ref, wo_ref, mq_ref, no_ref):
    n_mem, d = m_ref.shape
    dh = d // XA_HEADS
    mn = _rms(m_ref[...], g_ref[...]).astype(BF16)
    kv = _dot(mn, wkv_ref[0]).astype(BF16)
    for hd in range(XA_HEADS):
        k = kv[:, hd * dh:(hd + 1) * dh]
        v = kv[:, d + hd * dh:d + (hd + 1) * dh]
        mq = _dot_nt(wq_ref[0, :, hd * dh:(hd + 1) * dh], k) * (LOG2E * dh ** -0.5)
        mq_ref[0, 0, :, hd * n_mem:(hd + 1) * n_mem] = mq.astype(BF16)
        no_ref[0, 0, hd * n_mem:(hd + 1) * n_mem, :] = _dot(v, wo_ref[0, hd * dh:(hd + 1) * dh, :]).astype(BF16)


def _mem_fold(mem2d, g, wkv, wq, wo, b):
    bm, d = mem2d.shape
    n_mem = bm // b
    nl = wkv.shape[0]
    hm = XA_HEADS * n_mem
    return pl.pallas_call(
        _mem_fold_kernel,
        grid=(nl, b),
        in_specs=[
            pl.BlockSpec((n_mem, d), lambda l, bi: (bi, 0)),
            pl.BlockSpec((1, d), lambda l, bi: (0, 0)),
            pl.BlockSpec((1, d, 2 * d), lambda l, bi: (l, 0, 0)),
            pl.BlockSpec((1, d, d), lambda l, bi: (l, 0, 0)),
            pl.BlockSpec((1, d, d), lambda l, bi: (l, 0, 0)),
        ],
        out_specs=[pl.BlockSpec((1, 1, d, hm), lambda l, bi: (l, bi, 0, 0)),
                   pl.BlockSpec((1, 1, hm, d), lambda l, bi: (l, bi, 0, 0))],
        out_shape=[jax.ShapeDtypeStruct((nl, b, d, hm), BF16), jax.ShapeDtypeStruct((nl, b, hm, d), BF16)],
        compiler_params=_params("arbitrary", "arbitrary"),
        name="mem_fold",
    )(mem2d, g.reshape(1, d), wkv, wq, wo)


def _even_qkv_kernel(x_ref, g_ref, w_ref, qv_ref, kt_ref):
    hn = _rms(x_ref[0], g_ref[...]).astype(BF16)
    z = _dot(hn, w_ref[...])
    qv_ref[0, :, :NW] = (z[:, :NW] * (LOG2E * NA_HEAD_DIM ** -0.5)).astype(BF16)
    qv_ref[0, :, NW:] = z[:, 2 * NW:].astype(BF16)
    kt_ref[0] = z[:, NW:2 * NW].T.astype(BF16)


def _even_qkv(h, g, w_qkv, tm):
    b, s, d = h.shape
    n = w_qkv.shape[1]
    return pl.pallas_call(
        _even_qkv_kernel,
        grid=(b, s // tm),
        in_specs=[pl.BlockSpec((1, tm, d), lambda bi, i: (bi, i, 0)), _const_spec((1, d)), _const_spec((d, n))],
        out_specs=[pl.BlockSpec((1, tm, 2 * NW), lambda bi, i: (bi, i, 0)),
                   pl.BlockSpec((1, NW, tm), lambda bi, i: (bi, 0, i))],
        out_shape=[jax.ShapeDtypeStruct((b, s, 2 * NW), BF16), jax.ShapeDtypeStruct((b, NW, s), BF16)],
        compiler_params=_params("arbitrary", "arbitrary"),
        name="even_qkv_proj",
    )(h, g.reshape(1, d), w_qkv)


def _fourier_in_kernel(x_ref, g_ref, w_ref, cdft_ref, o_ref, sc_ref):
    no, d = x_ref.shape[1], x_ref.shape[3]
    gd = FOURIER_GROUP_DIM
    x = x_ref[0].reshape(no * SUBLANES, d)
    hn = _rms(x, g_ref[...]).astype(BF16)
    z = _dot(hn, w_ref[...])
    cd = cdft_ref[...]
    for grp in range(FOURIER_GROUPS):
        pq = _dot(z[:, grp * gd:(grp + 1) * gd].astype(BF16), cd)
        sc_ref[grp] = pq[:, :gd]
        sc_ref[FOURIER_GROUPS + grp] = pq[:, gd:]
    nslab = 2 * FOURIER_GROUPS
    for il in range(SUBLANES):
        rows = [sc_ref[sl, pl.ds(il, no, stride=SUBLANES), :] for sl in range(nslab)]
        o_ref[0, il] = jnp.concatenate(rows, axis=1).astype(BF16)


def _fourier_in(h, g, w_f, cdft):
    b, s, d = h.shape
    no = s // NI
    h4 = h.reshape(b, no, NI, d)
    return pl.pallas_call(
        _fourier_in_kernel,
        grid=(b, NI // SUBLANES),
        in_specs=[
            pl.BlockSpec((1, no, SUBLANES, d), lambda bi, i: (bi, 0, i, 0)),
            _const_spec((1, d)),
            _const_spec(w_f.shape),
            _const_spec(cdft.shape),
        ],
        out_specs=pl.BlockSpec((1, SUBLANES, no, 2 * FW), lambda bi, i: (bi, i, 0, 0)),
        out_shape=jax.ShapeDtypeStruct((b, NI, no, 2 * FW), BF16),
        scratch_shapes=[pltpu.VMEM((2 * FW // LANES, no * SUBLANES, LANES), F32)],
        compiler_params=_params("arbitrary", "arbitrary"),
        name="fourier_in_proj",
    )(h4, g.reshape(1, d), w_f, cdft)


def _fft_kernel(zt_ref, m1_ref, m3_ref, o_ref, bt_ref, yf_ref, *, pitch_b, pitch_y):
    no = zt_ref.shape[2]
    nslab = FW // LANES
    j = pl.program_id(1)

    @pl.when(j == 0)
    def _():
        def stage1(i, carry):
            zi = zt_ref[0, i]
            rhs = jnp.concatenate([zi[:, :FW], zi[:, FW:]], axis=0)
            a = _dot(m1_ref[i], rhs)
            row0 = pl.multiple_of(i * pitch_b, SUBLANES)
            for sl in range(nslab):
                bt_ref[sl, pl.ds(row0, 2 * no), :] = a[:, sl * LANES:(sl + 1) * LANES]
            return carry

        lax.fori_loop(0, NI, stage1, 0, unroll=32)

        def stage2(k1, carry):
            br = [bt_ref[sl, pl.ds(k1, NI, stride=pitch_b), :] for sl in range(nslab)]
            bi = [bt_ref[sl, pl.ds(no + k1, NI, stride=pitch_b), :] for sl in range(nslab)]
            rhs = jnp.concatenate([jnp.concatenate(br, axis=1).astype(BF16),
                                   jnp.concatenate(bi, axis=1).astype(BF16)], axis=0)
            y = _dot(m3_ref[...], rhs)
            for sl in range(nslab):
                yf_ref[sl, pl.ds(k1, NI, stride=pitch_y), :] = y[:, sl * LANES:(sl + 1) * LANES]
            return carry

        lax.fori_loop(0, no, stage2, 0, unroll=16)

    for k2l in range(SUBLANES):
        row0 = pl.multiple_of((j * SUBLANES + k2l) * pitch_y, SUBLANES)
        rows = jnp.concatenate([yf_ref[sl, pl.ds(row0, no), :] for sl in range(nslab)], axis=1)
        o_ref[0, k2l * no:(k2l + 1) * no, :] = rows.astype(BF16)


def _fft_mix(zt, m1, m3):
    b, _, no, _ = zt.shape
    s = NI * no
    tm = SUBLANES * no
    pitch_b = _strided_pitch(2 * no)
    pitch_y = _strided_pitch(no)
    return pl.pallas_call(
        functools.partial(_fft_kernel, pitch_b=pitch_b, pitch_y=pitch_y),
        grid=(b, NI // SUBLANES),
        in_specs=[
            pl.BlockSpec((1, NI, no, 2 * FW), lambda bi, j: (jnp.minimum(bi + (j > 0), b - 1), 0, 0, 0)),
            _const_spec(m1.shape),
            _const_spec(m3.shape),
        ],
        out_specs=pl.BlockSpec((1, tm, FW), lambda bi, j: (bi, j, 0)),
        out_shape=jax.ShapeDtypeStruct((b, s, FW), BF16),
        scratch_shapes=[
            pltpu.VMEM((FW // LANES, NI * pitch_b, LANES), F32),
            pltpu.VMEM((FW // LANES, NI * pitch_y, LANES), F32),
        ],
        compiler_params=_params("arbitrary", "arbitrary"),
        name="fft_mix",
    )(zt, m1, m3)


def _dft_constants(s):
    n = FOURIER_GROUP_DIM
    jn = jnp.arange(n, dtype=jnp.int32)
    ang = ((jn[:, None] * jn[None, :]) % n).astype(F32) * (2.0 * math.pi / n)
    cdft = (jnp.concatenate([jnp.cos(ang), jnp.sin(ang)], axis=1) * (n ** -0.5)).astype(BF16)
    no = s // NI
    i = jnp.arange(NI, dtype=jnp.int32)[:, None, None]
    k1 = jnp.arange(no, dtype=jnp.int32)[None, :, None]
    o = jnp.arange(no, dtype=jnp.int32)[None, None, :]
    ang1 = ((k1 * (NI * o + i)) % s).astype(F32) * (2.0 * math.pi / s)
    c1, s1 = jnp.cos(ang1), jnp.sin(ang1)
    m1 = jnp.concatenate([jnp.concatenate([c1, -s1], axis=2),
                          jnp.concatenate([-s1, -c1], axis=2)], axis=1).astype(BF16)
    ji = jnp.arange(NI, dtype=jnp.int32)
    ang3 = ((ji[:, None] * ji[None, :]) % NI).astype(F32) * (2.0 * math.pi / NI)
    m3 = (jnp.concatenate([jnp.cos(ang3), jnp.sin(ang3)], axis=1) * (s ** -0.5)).astype(BF16)
    return cdft, m1, m3


NA_QROWS = 4
NA_KROWS = NA_QROWS + NA_KH


def _na_kernel(pat_ref, q_ref, kt_ref, v_ref, bias_ref, yf_ref, x_ref, w_ref, o_ref, ya_ref, s_ref, *, rows):
    del pat_ref
    i = pl.program_id(1)
    krow0 = jnp.clip(i * NA_QROWS - NA_KH // 2, 0, rows - NA_KROWS)
    kstart = pl.multiple_of(krow0 * GRID_W, NA_QROWS * GRID_W)
    nk = NA_KROWS * GRID_W
    dh = NA_HEAD_DIM
    for hd in range(NA_HEADS):
        q = q_ref[0, :, hd * dh:(hd + 1) * dh]
        kt = kt_ref[0, hd * dh:(hd + 1) * dh, pl.ds(kstart, nk)]
        s_ref[hd] = _dot(q, kt) + bias_ref[0, hd]
    for hd in range(NA_HEADS):
        v = v_ref[0, pl.ds(kstart, nk), hd * dh:(hd + 1) * dh]
        ya_ref[:, hd * dh:(hd + 1) * dh] = _softmax_pv(s_ref[hd], v).astype(BF16)
    o_ref[0] = x_ref[0] + _dot(yf_ref[0], w_ref[:FW, :]) + _dot(ya_ref[...], w_ref[FW:, :])


def _na_bias_tables(rpb, rows):
    nblk = rows // NA_QROWS
    patterns, pat_of_block = [], []
    for blk in range(nblk):
        i0 = blk * NA_QROWS
        krow0 = min(max(i0 - NA_KH // 2, 0), rows - NA_KROWS)
        qi = np.arange(i0, i0 + NA_QROWS)
        rs = np.clip(qi - NA_KH // 2, 0, rows - NA_KH)
        key = (krow0 - i0, tuple((rs - i0).tolist()))
        if key not in patterns:
            patterns.append(key)
        pat_of_block.append(patterns.index(key))
    cols = np.arange(GRID_W)
    cs = np.clip(cols - NA_KW // 2, 0, GRID_W - NA_KW)
    col_valid = (cols[None, :] >= cs[:, None]) & (cols[None, :] < cs[:, None] + NA_KW)
    col_off = np.clip(cols[None, :] - cols[:, None] + NA_KW - 1, 0, 2 * NA_KW - 2)
    tables = []
    for dk, drs in patterns:
        a = np.arange(NA_QROWS)
        kr = dk + np.arange(NA_KROWS)
        rs_rel = np.asarray(drs)
        row_valid = (kr[None, :] >= rs_rel[:, None]) & (kr[None, :] < rs_rel[:, None] + NA_KH)
        row_off = np.clip(kr[None, :] - a[:, None] + NA_KH - 1, 0, 2 * NA_KH - 2)
        sel_r = (row_off[:, :, None] == np.arange(2 * NA_KH - 1)).astype(np.float32)
        sel_c = (col_off[:, :, None] == np.arange(2 * NA_KW - 1)).astype(np.float32)
        bias = jnp.einsum("hrk,amr,jck->hajmc", rpb, sel_r, sel_c, precision=lax.Precision.HIGHEST)
        valid = row_valid[:, None, :, None] & col_valid[None, :, None, :]
        bias = jnp.where(valid[None], bias * LOG2E, NEG_MASK)
        tables.append(bias.reshape(rpb.shape[0], NA_QROWS * GRID_W, NA_KROWS * GRID_W))
    return jnp.stack(tables).astype(F32), jnp.asarray(pat_of_block, jnp.int32)


def _na_mix(qv, kt, rpb, yf, h, w_out):
    b, s, d = h.shape
    rows = s // GRID_W
    assert rows >= NA_KROWS and rows % NA_QROWS == 0
    tables, pat = _na_bias_tables(rpb, rows)
    tq = NA_QROWS * GRID_W
    nk = NA_KROWS * GRID_W
    grid_spec = pltpu.PrefetchScalarGridSpec(
        num_scalar_prefetch=1,
        grid=(b, rows // NA_QROWS),
        in_specs=[
            pl.BlockSpec((1, tq, NW), lambda bi, i, p: (bi, i, 0)),
            pl.BlockSpec((1, NW, s), lambda bi, i, p: (bi, 0, 0)),
            pl.BlockSpec((1, s, NW), lambda bi, i, p: (bi, 0, 1)),
            pl.BlockSpec((1, NA_HEADS, tq, nk), lambda bi, i, p: (p[i], 0, 0, 0)),
            pl.BlockSpec((1, tq, FW), lambda bi, i, p: (bi, i, 0)),
            pl.BlockSpec((1, tq, d), lambda bi, i, p: (bi, i, 0)),
            pl.BlockSpec(w_out.shape, lambda bi, i, p: (0, 0), pipeline_mode=pl.Buffered(1)),
        ],
        out_specs=pl.BlockSpec((1, tq, d), lambda bi, i, p: (bi, i, 0)),
        scratch_shapes=[pltpu.VMEM((tq, NW), BF16), pltpu.VMEM((NA_HEADS, tq, nk), F32)],
    )
    return pl.pallas_call(
        functools.partial(_na_kernel, rows=rows),
        grid_spec=grid_spec,
        out_shape=jax.ShapeDtypeStruct(h.shape, F32),
        compiler_params=_params("arbitrary", "arbitrary"),
        name="neighborhood_attn",
    )(pat, qv, kt, qv, tables, yf, h, w_out)


def _halo_specs(tm, s, d):
    r = tm // HALO
    last_blk = s // HALO - 1

    def prev_map(b, i):
        return (b, jnp.maximum(i * r - 1, 0), 0)

    def next_map(b, i):
        return (b, jnp.minimum((i + 1) * r, last_blk), 0)

    return [
        pl.BlockSpec((1, tm, d), lambda b, i: (b, i, 0)),
        pl.BlockSpec((1, HALO, d), prev_map),
        pl.BlockSpec((1, HALO, d), next_map),
    ]


def _fill_normed_ext(hn_ref, x, xp, xn, g):
    tm = x.shape[0]
    i = pl.program_id(1)
    first = i == 0
    last = i == pl.num_programs(1) - 1
    hp = jnp.where(first, 0.0, _rms(xp, g))
    hx = jnp.where(last, 0.0, _rms(xn, g))
    hn_ref[:HALO, :] = hp.astype(BF16)
    hn_ref[HALO:HALO + tm, :] = _rms(x, g).astype(BF16)
    hn_ref[HALO + tm:, :] = hx.astype(BF16)


def _conv3_rows(z, cw, tm):
    n = z.shape[0]
    prev = pltpu.roll(z, 1, 0)[HALO:HALO + tm]
    nxt = pltpu.roll(z, n - 1, 0)[HALO:HALO + tm]
    return prev * cw[0:1, :] + z[HALO:HALO + tm] * cw[1:2, :] + nxt * cw[2:3, :]


def _odd_mixer_kernel(x_ref, xp_ref, xn_ref, g_ref, win_ref, cw_ref, wout_ref, o_ref, hn_ref, *, nc):
    tm, d = x_ref.shape[1], x_ref.shape[2]
    sub = min(SUB_TILE, tm)
    _fill_normed_ext(hn_ref, x_ref[0], xp_ref[0], xn_ref[0], g_ref[...])
    nchunks = d // nc
    for t in range(tm // sub):
        r0 = t * sub
        hn_ext = hn_ref[r0:r0 + sub + 2 * HALO, :]
        hn = hn_ref[HALO + r0:HALO + r0 + sub, :]

        def project(c):
            lo, hi = c * nc, (c + 1) * nc
            return (_dot(hn, win_ref[:, lo:hi]),
                    _dot(hn_ext, win_ref[:, d + lo:d + hi]),
                    _dot(hn_ext, win_ref[:, 2 * d + lo:2 * d + hi]))

        acc = x_ref[0, r0:r0 + sub, :]
        cur = project(0)
        for c in range(nchunks):
            nxt = project(c + 1) if c + 1 < nchunks else None
            gate_b, gate_c, u = cur
            lo, hi = c * nc, (c + 1) * nc
            y = gate_b * _conv3_rows(gate_c * u, cw_ref[:, lo:hi], sub)
            acc = acc + _dot(y.astype(BF16), wout_ref[lo:hi, :])
            cur = nxt
        o_ref[0, r0:r0 + sub, :] = acc


def _odd_mixer(h, g, w_in, conv_w, w_out, tm, nc=256):
    b, s, d = h.shape
    return pl.pallas_call(
        functools.partial(_odd_mixer_kernel, nc=nc),
        grid=(b, s // tm),
        in_specs=_halo_specs(tm, s, d) + [
            _const_spec((1, d)),
            _const_spec(w_in.shape),
            _const_spec(conv_w.shape),
            _const_spec(w_out.shape),
        ],
        out_specs=pl.BlockSpec((1, tm, d), lambda bi, i: (bi, i, 0)),
        out_shape=jax.ShapeDtypeStruct(h.shape, F32),
        scratch_shapes=[pltpu.VMEM((tm + 2 * HALO, d), BF16)],
        compiler_params=_params("arbitrary", "arbitrary"),
        name="odd_mixer",
    )(h, h, h, g.reshape(1, d), w_in, conv_w, w_out)


def _xattn_kernel(x_ref, g_ref, mq_ref, no_ref, o_ref, p_ref, s_ref):
    n_mem = mq_ref.shape[3] // XA_HEADS
    tm = x_ref.shape[1]
    sub = min(SUB_TILE, tm)
    for t in range(tm // sub):
        r0 = t * sub
        x = x_ref[0, r0:r0 + sub, :]
        hn = _rms(x, g_ref[...]).astype(BF16)
        s_ref[r0:r0 + sub, :] = _dot(hn, mq_ref[0, 0])
        for hd in range(XA_HEADS):
            s = s_ref[r0:r0 + sub, hd * n_mem:(hd + 1) * n_mem]
            p = jnp.exp2(s - jnp.max(s, axis=-1, keepdims=True))
            r = 1.0 / jnp.sum(p, axis=-1, keepdims=True)
            p_ref[r0:r0 + sub, hd * n_mem:(hd + 1) * n_mem] = (p * r).astype(BF16)
        o_ref[0, r0:r0 + sub, :] = x + _dot(p_ref[r0:r0 + sub, :], no_ref[0, 0])


def _xattn(h, g, mq, no, layer, tm):
    b, s, d = h.shape
    hm = mq.shape[3]
    return pl.pallas_call(
        _xattn_kernel,
        grid=(b, s // tm),
        in_specs=[
            pl.BlockSpec((1, tm, d), lambda bi, i: (bi, i, 0)),
            _const_spec((1, d)),
            pl.BlockSpec((1, 1, d, hm), lambda bi, i: (layer, bi, 0, 0)),
            pl.BlockSpec((1, 1, hm, d), lambda bi, i: (layer, bi, 0, 0)),
        ],
        out_specs=pl.BlockSpec((1, tm, d), lambda bi, i: (bi, i, 0)),
        out_shape=jax.ShapeDtypeStruct(h.shape, F32),
        scratch_shapes=[pltpu.VMEM((tm, hm), BF16), pltpu.VMEM((tm, hm), F32)],
        compiler_params=_params("arbitrary", "arbitrary"),
        name="mem_xattn",
    )(h, g.reshape(1, d), mq, no)


def _gelu_exact(x):
    return 0.5 * x * (1.0 + lax.erf(x * (1.0 / math.sqrt(2.0))))


def _ffn_kernel(x_ref, xp_ref, xn_ref, g_ref, wup_ref, cw_ref, cb_ref, wdn_ref, gf_ref, o_ref,
                hn_ref, *, nc, final_norm):
    tm = x_ref.shape[1]
    dff = wdn_ref.shape[0]
    sub = min(SUB_TILE, tm)
    _fill_normed_ext(hn_ref, x_ref[0], xp_ref[0], xn_ref[0], g_ref[...])
    nchunks = dff // nc
    for t in range(tm // sub):
        r0 = t * sub
        hn_ext = hn_ref[r0:r0 + sub + 2 * HALO, :]
        hn = hn_ref[HALO + r0:HALO + r0 + sub, :]

        def project(c):
            lo, hi = c * nc, (c + 1) * nc
            return _dot(hn, wup_ref[:, lo:hi]), _dot(hn_ext, wup_ref[:, dff + lo:dff + hi])

        acc = x_ref[0, r0:r0 + sub, :]
        cur = project(0)
        for c in range(nchunks):
            nxt = project(c + 1) if c + 1 < nchunks else None
            u, gp = cur
            lo, hi = c * nc, (c + 1) * nc
            gate = _conv3_rows(gp, cw_ref[:, lo:hi], sub) + cb_ref[:, lo:hi]
            a = _gelu_exact(gate) * u
            acc = acc + _dot(a.astype(BF16), wdn_ref[lo:hi, :])
            cur = nxt
        if final_norm:
            acc = _rms(acc, gf_ref[...])
        o_ref[0, r0:r0 + sub, :] = acc


def _conv_ffn(h, g, w_up, conv_w, conv_b, w_down, final_g, tm, nc=256):
    b, s, d = h.shape
    dff = w_down.shape[0]
    final_norm = final_g is not None
    gf = (final_g if final_norm else g).reshape(1, d)
    return pl.pallas_call(
        functools.partial(_ffn_kernel, nc=nc, final_norm=final_norm),
        grid=(b, s // tm),
        in_specs=_halo_specs(tm, s, d) + [
            _const_spec((1, d)),
            _const_spec(w_up.shape),
            _const_spec(conv_w.shape),
            _const_spec((1, dff)),
            _const_spec(w_down.shape),
            _const_spec((1, d)),
        ],
        out_specs=pl.BlockSpec((1, tm, d), lambda bi, i: (bi, i, 0)),
        out_shape=jax.ShapeDtypeStruct(h.shape, F32),
        scratch_shapes=[pltpu.VMEM((tm + 2 * HALO, d), BF16)],
        compiler_params=_params("arbitrary", "arbitrary"),
        name="conv_ffn",
    )(h, h, h, g.reshape(1, d), w_up, conv_w, conv_b.reshape(1, dff), w_down, gf)


def kernel(x, mem, mem_norm_g, mix_norm_g, w_in_ab, rpb, w_out_ab, w_in_c, conv_c, w_out_c,
           xa_norm_g, xa_wq, xa_wkv, xa_wo, ffn_norm_g, ffn_w_up, ffn_conv_w, ffn_conv_b,
           ffn_w_down, final_norm_g):
    b, s, d = x.shape
    depth = mix_norm_g.shape[0]
    n_mem = mem.shape[1]
    tm = min(SUB_TILE, s)
    tt = min(TOKEN_TILE, s)
    bf = lambda w: w.astype(BF16)

    cdft, m1, m3 = _dft_constants(s)
    mq, no = _mem_fold(mem.reshape(b * n_mem, d), mem_norm_g, bf(xa_wkv), bf(xa_wq), bf(xa_wo), b)

    h = x
    for layer in range(depth):
        j = layer // 2
        if layer % 2 == 0:
            g = mix_norm_g[layer]
            zt = _fourier_in(h, g, bf(w_in_ab[j, :, :FW]), cdft)
            qv, kt = _even_qkv(h, g, bf(w_in_ab[j, :, FW:]), tm)
            yf = _fft_mix(zt, m1, m3)
            h = _na_mix(qv, kt, rpb[j], yf, h, bf(w_out_ab[j]))
        else:
            h = _odd_mixer(h, mix_norm_g[layer], bf(w_in_c[j]), conv_c[j], bf(w_out_c[j]), tt)
        h = _xattn(h, xa_norm_g[layer], mq, no, layer, tt)
        h = _conv_ffn(h, ffn_norm_g[layer], bf(ffn_w_up[layer]), ffn_conv_w[layer], ffn_conv_b[layer],
                      bf(ffn_w_down[layer]), final_norm_g if layer == depth - 1 else None, tt)
    return h
```

```python
import functools
import math

import numpy as np
import jax
import jax.numpy as jnp
from jax import lax
from jax.experimental import pallas as pl
from jax.experimental.pallas import tpu as pltpu

EPS = 1e-6
GRID_W = 64
NA_KH, NA_KW = 8, 16
NA_HEADS = 4
NA_HEAD_DIM = 128
FOURIER_GROUPS = 4
FOURIER_GROUP_DIM = 128
XA_HEADS = 4
NEG_MASK = -1e30

LANES = 128
SUBLANES = 8
HALO = 16
SUB_TILE = 512
TOKEN_TILE = 1024
VMEM_LIMIT = 56 * 1024 * 1024

F32 = jnp.float32
BF16 = jnp.bfloat16

FW = FOURIER_GROUPS * FOURIER_GROUP_DIM
NW = NA_HEADS * NA_HEAD_DIM
NI = GRID_W


def _rms(x, g):
    ms = jnp.mean(x * x, axis=-1, keepdims=True)
    return (x * lax.rsqrt(ms + EPS)) * g


def _dot(a, b):
    return jnp.dot(a, b, preferred_element_type=F32)


def _dot_nt(a, b):
    return lax.dot_general(a, b, (((1,), (1,)), ((), ())), preferred_element_type=F32)


LOG2E = 1.4426950408889634


def _softmax_pv(s, v):
    m = jnp.max(s, axis=-1, keepdims=True)
    p = jnp.exp2(s - m)
    l = jnp.sum(p, axis=-1, keepdims=True)
    return _dot(p.astype(BF16), v) / l


def _const_spec(shape):
    nd = len(shape)
    return pl.BlockSpec(shape, lambda *_: (0,) * nd, pipeline_mode=pl.Buffered(1))


def _params(*sem):
    return pltpu.CompilerParams(dimension_semantics=sem, vmem_limit_bytes=VMEM_LIMIT)


def _strided_pitch(n):
    p = -(-n // SUBLANES) * SUBLANES
    return p if (p // SUBLANES) % 2 == 1 else p + SUBLANES


def _mem_fold_kernel(m_ref, g_ref, wkv_ref, wq_ref, wo_ref, mq_ref, no_ref):
    n_mem, d = m_ref.shape
    dh = d // XA_HEADS
    mn = _rms(m_ref[...], g_ref[...]).astype(BF16)
    kv = _dot(mn, wkv_ref[0]).astype(BF16)
    for hd in range(XA_HEADS):
        k = kv[:, hd * dh:(hd + 1) * dh]
        v = kv[:, d + hd * dh:d + (hd + 1) * dh]
        mq = _dot_nt(wq_ref[0, :, hd * dh:(hd + 1) * dh], k) * (LOG2E * dh ** -0.5)
        mq_ref[0, 0, :, hd * n_mem:(hd + 1) * n_mem] = mq.astype(BF16)
        no_ref[0, 0, hd * n_mem:(hd + 1) * n_mem, :] = _dot(v, wo_ref[0, hd * dh:(hd + 1) * dh, :]).astype(BF16)


def _mem_fold(mem2d, g, wkv, wq, wo, b):
    bm, d = mem2d.shape
    n_mem = bm // b
    nl = wkv.shape[0]
    hm = XA_HEADS * n_mem
    return pl.pallas_call(
        _mem_fold_kernel,
        grid=(nl, b),
        in_specs=[
            pl.BlockSpec((n_mem, d), lambda l, bi: (bi, 0)),
            pl.BlockSpec((1, d), lambda l, bi: (0, 0)),
            pl.BlockSpec((1, d, 2 * d), lambda l, bi: (l, 0, 0)),
            pl.BlockSpec((1, d, d), lambda l, bi: (l, 0, 0)),
            pl.BlockSpec((1, d, d), lambda l, bi: (l, 0, 0)),
        ],
        out_specs=[pl.BlockSpec((1, 1, d, hm), lambda l, bi: (l, bi, 0, 0)),
                   pl.BlockSpec((1, 1, hm, d), lambda l, bi: (l, bi, 0, 0))],
        out_shape=[jax.ShapeDtypeStruct((nl, b, d, hm), BF16), jax.ShapeDtypeStruct((nl, b, hm, d), BF16)],
        compiler_params=_params("arbitrary", "arbitrary"),
        name="mem_fold",
    )(mem2d, g.reshape(1, d), wkv, wq, wo)


def _even_qkv_kernel(x_ref, g_ref, w_ref, qv_ref, kt_ref):
    hn = _rms(x_ref[0], g_ref[...]).astype(BF16)
    z = _dot(hn, w_ref[...])
    qv_ref[0, :, :NW] = (z[:, :NW] * (LOG2E * NA_HEAD_DIM ** -0.5)).astype(BF16)
    qv_ref[0, :, NW:] = z[:, 2 * NW:].astype(BF16)
    kt_ref[0] = z[:, NW:2 * NW].T.astype(BF16)


def _even_qkv(h, g, w_qkv, tm):
    b, s, d = h.shape
    n = w_qkv.shape[1]
    return pl.pallas_call(
        _even_qkv_kernel,
        grid=(b, s // tm),
        in_specs=[pl.BlockSpec((1, tm, d), lambda bi, i: (bi, i, 0)), _const_spec((1, d)), _const_spec((d, n))],
        out_specs=[pl.BlockSpec((1, tm, 2 * NW), lambda bi, i: (bi, i, 0)),
                   pl.BlockSpec((1, NW, tm), lambda bi, i: (bi, 0, i))],
        out_shape=[jax.ShapeDtypeStruct((b, s, 2 * NW), BF16), jax.ShapeDtypeStruct((b, NW, s), BF16)],
        compiler_params=_params("arbitrary", "arbitrary"),
        name="even_qkv_proj",
    )(h, g.reshape(1, d), w_qkv)


def _fourier_in_kernel(x_ref, g_ref, w_ref, cdft_ref, o_ref, sc_ref):
    no, d = x_ref.shape[1], x_ref.shape[3]
    gd = FOURIER_GROUP_DIM
    x = x_ref[0].reshape(no * SUBLANES, d)
    hn = _rms(x, g_ref[...]).astype(BF16)
    z = _dot(hn, w_ref[...])
    cd = cdft_ref[...]
    for grp in range(FOURIER_GROUPS):
        pq = _dot(z[:, grp * gd:(grp + 1) * gd].astype(BF16), cd)
        sc_ref[grp] = pq[:, :gd]
        sc_ref[FOURIER_GROUPS + grp] = pq[:, gd:]
    nslab = 2 * FOURIER_GROUPS
    for il in range(SUBLANES):
        rows = [sc_ref[sl, pl.ds(il, no, stride=SUBLANES), :] for sl in range(nslab)]
        o_ref[0, il] = jnp.concatenate(rows, axis=1).astype(BF16)


def _fourier_in(h, g, w_f, cdft):
    b, s, d = h.shape
    no = s // NI
    h4 = h.reshape(b, no, NI, d)
    return pl.pallas_call(
        _fourier_in_kernel,
        grid=(b, NI // SUBLANES),
        in_specs=[
            pl.BlockSpec((1, no, SUBLANES, d), lambda bi, i: (bi, 0, i, 0)),
            _const_spec((1, d)),
            _const_spec(w_f.shape),
            _const_spec(cdft.shape),
        ],
        out_specs=pl.BlockSpec((1, SUBLANES, no, 2 * FW), lambda bi, i: (bi, i, 0, 0)),
        out_shape=jax.ShapeDtypeStruct((b, NI, no, 2 * FW), BF16),
        scratch_shapes=[pltpu.VMEM((2 * FW // LANES, no * SUBLANES, LANES), F32)],
        compiler_params=_params("arbitrary", "arbitrary"),
        name="fourier_in_proj",
    )(h4, g.reshape(1, d), w_f, cdft)


def _fft_kernel(zt_ref, m1_ref, m3_ref, o_ref, bt_ref, yf_ref, *, pitch_b, pitch_y):
    no = zt_ref.shape[2]
    nslab = FW // LANES
    j = pl.program_id(1)

    @pl.when(j == 0)
    def _():
        def stage1(i, carry):
            zi = zt_ref[0, i]
            rhs = jnp.concatenate([zi[:, :FW], zi[:, FW:]], axis=0)
            a = _dot(m1_ref[i], rhs)
            row0 = pl.multiple_of(i * pitch_b, SUBLANES)
            for sl in range(nslab):
                bt_ref[sl, pl.ds(row0, 2 * no), :] = a[:, sl * LANES:(sl + 1) * LANES]
            return carry

        lax.fori_loop(0, NI, stage1, 0, unroll=32)

        def stage2(k1, carry):
            br = [bt_ref[sl, pl.ds(k1, NI, stride=pitch_b), :] for sl in range(nslab)]
            bi = [bt_ref[sl, pl.ds(no + k1, NI, stride=pitch_b), :] for sl in range(nslab)]
            rhs = jnp.concatenate([jnp.concatenate(br, axis=1).astype(BF16),
                                   jnp.concatenate(bi, axis=1).astype(BF16)], axis=0)
            y = _dot(m3_ref[...], rhs)
            for sl in range(nslab):
                yf_ref[sl, pl.ds(k1, NI, stride=pitch_y), :] = y[:, sl * LANES:(sl + 1) * LANES]
            return carry

        lax.fori_loop(0, no, stage2, 0, unroll=16)

    for k2l in range(SUBLANES):
        row0 = pl.multiple_of((j * SUBLANES + k2l) * pitch_y, SUBLANES)
        rows = jnp.concatenate([yf_ref[sl, pl.ds(row0, no), :] for sl in range(nslab)], axis=1)
        o_ref[0, k2l * no:(k2l + 1) * no, :] = rows.astype(BF16)


def _fft_mix(zt, m1, m3):
    b, _, no, _ = zt.shape
    s = NI * no
    tm = SUBLANES * no
    pitch_b = _strided_pitch(2 * no)
    pitch_y = _strided_pitch(no)
    return pl.pallas_call(
        functools.partial(_fft_kernel, pitch_b=pitch_b, pitch_y=pitch_y),
        grid=(b, NI // SUBLANES),
        in_specs=[
            pl.BlockSpec((1, NI, no, 2 * FW), lambda bi, j: (jnp.minimum(bi + (j > 0), b - 1), 0, 0, 0)),
            _const_spec(m1.shape),
            _const_spec(m3.shape),
        ],
        out_specs=pl.BlockSpec((1, tm, FW), lambda bi, j: (bi, j, 0)),
        out_shape=jax.ShapeDtypeStruct((b, s, FW), BF16),
        scratch_shapes=[
            pltpu.VMEM((FW // LANES, NI * pitch_b, LANES), F32),
            pltpu.VMEM((FW // LANES, NI * pitch_y, LANES), F32),
        ],
        compiler_params=_params("arbitrary", "arbitrary"),
        name="fft_mix",
    )(zt, m1, m3)


def _dft_constants(s):
    n = FOURIER_GROUP_DIM
    jn = jnp.arange(n, dtype=jnp.int32)
    ang = ((jn[:, None] * jn[None, :]) % n).astype(F32) * (2.0 * math.pi / n)
    cdft = (jnp.concatenate([jnp.cos(ang), jnp.sin(ang)], axis=1) * (n ** -0.5)).astype(BF16)
    no = s // NI
    i = jnp.arange(NI, dtype=jnp.int32)[:, None, None]
    k1 = jnp.arange(no, dtype=jnp.int32)[None, :, None]
    o = jnp.arange(no, dtype=jnp.int32)[None, None, :]
    ang1 = ((k1 * (NI * o + i)) % s).astype(F32) * (2.0 * math.pi / s)
    c1, s1 = jnp.cos(ang1), jnp.sin(ang1)
    m1 = jnp.concatenate([jnp.concatenate([c1, -s1], axis=2),
                          jnp.concatenate([-s1, -c1], axis=2)], axis=1).astype(BF16)
    ji = jnp.arange(NI, dtype=jnp.int32)
    ang3 = ((ji[:, None] * ji[None, :]) % NI).astype(F32) * (2.0 * math.pi / NI)
    m3 = (jnp.concatenate([jnp.cos(ang3), jnp.sin(ang3)], axis=1) * (s ** -0.5)).astype(BF16)
    return cdft, m1, m3


NA_QROWS = 4
NA_KROWS = NA_QROWS + NA_KH


def _na_kernel(pat_ref, q_ref, kt_ref, v_ref, bias_ref, yf_ref, x_ref, w_ref, o_ref, ya_ref, s_ref, *, rows):
    del pat_ref
    i = pl.program_id(1)
    krow0 = jnp.clip(i * NA_QROWS - NA_KH // 2, 0, rows - NA_KROWS)
    kstart = pl.multiple_of(krow0 * GRID_W, NA_QROWS * GRID_W)
    nk = NA_KROWS * GRID_W
    dh = NA_HEAD_DIM
    for hd in range(NA_HEADS):
        q = q_ref[0, :, hd * dh:(hd + 1) * dh]
        kt = kt_ref[0, hd * dh:(hd + 1) * dh, pl.ds(kstart, nk)]
        s_ref[hd] = _dot(q, kt) + bias_ref[0, hd]
    for hd in range(NA_HEADS):
        v = v_ref[0, pl.ds(kstart, nk), hd * dh:(hd + 1) * dh]
        ya_ref[:, hd * dh:(hd + 1) * dh] = _softmax_pv(s_ref[hd], v).astype(BF16)
    o_ref[0] = x_ref[0] + _dot(yf_ref[0], w_ref[:FW, :]) + _dot(ya_ref[...], w_ref[FW:, :])


def _na_bias_tables(rpb, rows):
    nblk = rows // NA_QROWS
    patterns, pat_of_block = [], []
    for blk in range(nblk):
        i0 = blk * NA_QROWS
        krow0 = min(max(i0 - NA_KH // 2, 0), rows - NA_KROWS)
        qi = np.arange(i0, i0 + NA_QROWS)
        rs = np.clip(qi - NA_KH // 2, 0, rows - NA_KH)
        key = (krow0 - i0, tuple((rs - i0).tolist()))
        if key not in patterns:
            patterns.append(key)
        pat_of_block.append(patterns.index(key))
    cols = np.arange(GRID_W)
    cs = np.clip(cols - NA_KW // 2, 0, GRID_W - NA_KW)
    col_valid = (cols[None, :] >= cs[:, None]) & (cols[None, :] < cs[:, None] + NA_KW)
    col_off = np.clip(cols[None, :] - cols[:, None] + NA_KW - 1, 0, 2 * NA_KW - 2)
    tables = []
    for dk, drs in patterns:
        a = np.arange(NA_QROWS)
        kr = dk + np.arange(NA_KROWS)
        rs_rel = np.asarray(drs)
        row_valid = (kr[None, :] >= rs_rel[:, None]) & (kr[None, :] < rs_rel[:, None] + NA_KH)
        row_off = np.clip(kr[None, :] - a[:, None] + NA_KH - 1, 0, 2 * NA_KH - 2)
        sel_r = (row_off[:, :, None] == np.arange(2 * NA_KH - 1)).astype(np.float32)
        sel_c = (col_off[:, :, None] == np.arange(2 * NA_KW - 1)).astype(np.float32)
        bias = jnp.einsum("hrk,amr,jck->hajmc", rpb, sel_r, sel_c, precision=lax.Precision.HIGHEST)
        valid = row_valid[:, None, :, None] & col_valid[None, :, None, :]
        bias = jnp.where(valid[None], bias * LOG2E, NEG_MASK)
        tables.append(bias.reshape(rpb.shape[0], NA_QROWS * GRID_W, NA_KROWS * GRID_W))
    return jnp.stack(tables).astype(F32), jnp.asarray(pat_of_block, jnp.int32)


def _na_mix(qv, kt, rpb, yf, h, w_out):
    b, s, d = h.shape
    rows = s // GRID_W
    assert rows >= NA_KROWS and rows % NA_QROWS == 0
    tables, pat = _na_bias_tables(rpb, rows)
    tq = NA_QROWS * GRID_W
    nk = NA_KROWS * GRID_W
    grid_spec = pltpu.PrefetchScalarGridSpec(
        num_scalar_prefetch=1,
        grid=(b, rows // NA_QROWS),
        in_specs=[
            pl.BlockSpec((1, tq, NW), lambda bi, i, p: (bi, i, 0)),
            pl.BlockSpec((1, NW, s), lambda bi, i, p: (bi, 0, 0)),
            pl.BlockSpec((1, s, NW), lambda bi, i, p: (bi, 0, 1)),
            pl.BlockSpec((1, NA_HEADS, tq, nk), lambda bi, i, p: (p[i], 0, 0, 0)),
            pl.BlockSpec((1, tq, FW), lambda bi, i, p: (bi, i, 0)),
            pl.BlockSpec((1, tq, d), lambda bi, i, p: (bi, i, 0)),
            pl.BlockSpec(w_out.shape, lambda bi, i, p: (0, 0), pipeline_mode=pl.Buffered(1)),
        ],
        out_specs=pl.BlockSpec((1, tq, d), lambda bi, i, p: (bi, i, 0)),
        scratch_shapes=[pltpu.VMEM((tq, NW), BF16), pltpu.VMEM((NA_HEADS, tq, nk), F32)],
    )
    return pl.pallas_call(
        functools.partial(_na_kernel, rows=rows),
        grid_spec=grid_spec,
        out_shape=jax.ShapeDtypeStruct(h.shape, F32),
        compiler_params=_params("arbitrary", "arbitrary"),
        name="neighborhood_attn",
    )(pat, qv, kt, qv, tables, yf, h, w_out)


def _halo_specs(tm, s, d):
    r = tm // HALO
    last_blk = s // HALO - 1

    def prev_map(b, i):
        return (b, jnp.maximum(i * r - 1, 0), 0)

    def next_map(b, i):
        return (b, jnp.minimum((i + 1) * r, last_blk), 0)

    return [
        pl.BlockSpec((1, tm, d), lambda b, i: (b, i, 0)),
        pl.BlockSpec((1, HALO, d), prev_map),
        pl.BlockSpec((1, HALO, d), next_map),
    ]


def _fill_normed_ext(hn_ref, x, xp, xn, g):
    tm = x.shape[0]
    i = pl.program_id(1)
    first = i == 0
    last = i == pl.num_programs(1) - 1
    hp = jnp.where(first, 0.0, _rms(xp, g))
    hx = jnp.where(last, 0.0, _rms(xn, g))
    hn_ref[:HALO, :] = hp.astype(BF16)
    hn_ref[HALO:HALO + tm, :] = _rms(x, g).astype(BF16)
    hn_ref[HALO + tm:, :] = hx.astype(BF16)


def _conv3_rows(z, cw, tm):
    n = z.shape[0]
    prev = pltpu.roll(z, 1, 0)[HALO:HALO + tm]
    nxt = pltpu.roll(z, n - 1, 0)[HALO:HALO + tm]
    return prev * cw[0:1, :] + z[HALO:HALO + tm] * cw[1:2, :] + nxt * cw[2:3, :]


def _odd_mixer_kernel(x_ref, xp_ref, xn_ref, g_ref, win_ref, cw_ref, wout_ref, o_ref, hn_ref, y_ref, *, nc):
    tm, d = x_ref.shape[1], x_ref.shape[2]
    sub = min(SUB_TILE, tm)
    _fill_normed_ext(hn_ref, x_ref[0], xp_ref[0], xn_ref[0], g_ref[...])
    nchunks = d // nc
    for t in range(tm // sub):
        r0 = t * sub
        hn_ext = hn_ref[r0:r0 + sub + 2 * HALO, :]
        hn = hn_ref[HALO + r0:HALO + r0 + sub, :]

        def project(c):
            lo, hi = c * nc, (c + 1) * nc
            return (_dot(hn, win_ref[:, lo:hi]),
                    _dot(hn_ext, win_ref[:, d + lo:d + hi]),
                    _dot(hn_ext, win_ref[:, 2 * d + lo:2 * d + hi]))

        cur = project(0)
        for c in range(nchunks):
            nxt = project(c + 1) if c + 1 < nchunks else None
            gate_b, gate_c, u = cur
            lo, hi = c * nc, (c + 1) * nc
            y_ref[r0:r0 + sub, lo:hi] = (gate_b * _conv3_rows(gate_c * u, cw_ref[:, lo:hi], sub)).astype(BF16)
            cur = nxt
        o_ref[0, r0:r0 + sub, :] = x_ref[0, r0:r0 + sub, :] + _dot(y_ref[r0:r0 + sub, :], wout_ref[...])


def _odd_mixer(h, g, w_in, conv_w, w_out, tm, nc=256):
    b, s, d = h.shape
    return pl.pallas_call(
        functools.partial(_odd_mixer_kernel, nc=nc),
        grid=(b, s // tm),
        in_specs=_halo_specs(tm, s, d) + [
            _const_spec((1, d)),
            _const_spec(w_in.shape),
            _const_spec(conv_w.shape),
            _const_spec(w_out.shape),
        ],
        out_specs=pl.BlockSpec((1, tm, d), lambda bi, i: (bi, i, 0)),
        out_shape=jax.ShapeDtypeStruct(h.shape, F32),
        scratch_shapes=[pltpu.VMEM((tm + 2 * HALO, d), BF16), pltpu.VMEM((tm, d), BF16)],
        compiler_params=_params("arbitrary", "arbitrary"),
        name="odd_mixer",
    )(h, h, h, g.reshape(1, d), w_in, conv_w, w_out)


def _xattn_kernel(x_ref, g_ref, mq_ref, no_ref, o_ref, p_ref, s_ref):
    n_mem = mq_ref.shape[3] // XA_HEADS
    tm = x_ref.shape[1]
    sub = min(SUB_TILE, tm)
    for t in range(tm // sub):
        r0 = t * sub
        x = x_ref[0, r0:r0 + sub, :]
        hn = _rms(x, g_ref[...]).astype(BF16)
        s_ref[r0:r0 + sub, :] = _dot(hn, mq_ref[0, 0])
        for hd in range(XA_HEADS):
            s = s_ref[r0:r0 + sub, hd * n_mem:(hd + 1) * n_mem]
            p = jnp.exp2(s - jnp.max(s, axis=-1, keepdims=True))
            r = 1.0 / jnp.sum(p, axis=-1, keepdims=True)
            p_ref[r0:r0 + sub, hd * n_mem:(hd + 1) * n_mem] = (p * r).astype(BF16)
        o_ref[0, r0:r0 + sub, :] = x + _dot(p_ref[r0:r0 + sub, :], no_ref[0, 0])


def _xattn(h, g, mq, no, layer, tm):
    b, s, d = h.shape
    hm = mq.shape[3]
    return pl.pallas_call(
        _xattn_kernel,
        grid=(b, s // tm),
        in_specs=[
            pl.BlockSpec((1, tm, d), lambda bi, i: (bi, i, 0)),
            _const_spec((1, d)),
            pl.BlockSpec((1, 1, d, hm), lambda bi, i: (layer, bi, 0, 0)),
            pl.BlockSpec((1, 1, hm, d), lambda bi, i: (layer, bi, 0, 0)),
        ],
        out_specs=pl.BlockSpec((1, tm, d), lambda bi, i: (bi, i, 0)),
        out_shape=jax.ShapeDtypeStruct(h.shape, F32),
        scratch_shapes=[pltpu.VMEM((tm, hm), BF16), pltpu.VMEM((tm, hm), F32)],
        compiler_params=_params("arbitrary", "arbitrary"),
        name="mem_xattn",
    )(h, g.reshape(1, d), mq, no)


def _gelu_exact(x):
    return 0.5 * x * (1.0 + lax.erf(x * (1.0 / math.sqrt(2.0))))


def _ffn_kernel(x_ref, xp_ref, xn_ref, g_ref, wup_ref, cw_ref, cb_ref, wdn_ref, gf_ref, o_ref,
                hn_ref, a_ref, *, nc, final_norm):
    tm = x_ref.shape[1]
    dff = wdn_ref.shape[0]
    sub = min(SUB_TILE, tm)
    _fill_normed_ext(hn_ref, x_ref[0], xp_ref[0], xn_ref[0], g_ref[...])
    nchunks = dff // nc
    for t in range(tm // sub):
        r0 = t * sub
        hn_ext = hn_ref[r0:r0 + sub + 2 * HALO, :]
        hn = hn_ref[HALO + r0:HALO + r0 + sub, :]

        def project(c):
            lo, hi = c * nc, (c + 1) * nc
            return _dot(hn, wup_ref[:, lo:hi]), _dot(hn_ext, wup_ref[:, dff + lo:dff + hi])

        cur = project(0)
        for c in range(nchunks):
            nxt = project(c + 1) if c + 1 < nchunks else None
            u, gp = cur
            lo, hi = c * nc, (c + 1) * nc
            gate = _conv3_rows(gp, cw_ref[:, lo:hi], sub) + cb_ref[:, lo:hi]
            a_ref[r0:r0 + sub, lo:hi] = (_gelu_exact(gate) * u).astype(BF16)
            cur = nxt
        acc = x_ref[0, r0:r0 + sub, :] + _dot(a_ref[r0:r0 + sub, :], wdn_ref[...])
        if final_norm:
            acc = _rms(acc, gf_ref[...])
        o_ref[0, r0:r0 + sub, :] = acc


def _conv_ffn(h, g, w_up, conv_w, conv_b, w_down, final_g, tm, nc=256):
    b, s, d = h.shape
    dff = w_down.shape[0]
    final_norm = final_g is not None
    gf = (final_g if final_norm else g).reshape(1, d)
    return pl.pallas_call(
        functools.partial(_ffn_kernel, nc=nc, final_norm=final_norm),
        grid=(b, s // tm),
        in_specs=_halo_specs(tm, s, d) + [
            _const_spec((1, d)),
            _const_spec(w_up.shape),
            _const_spec(conv_w.shape),
            _const_spec((1, dff)),
            _const_spec(w_down.shape),
            _const_spec((1, d)),
        ],
        out_specs=pl.BlockSpec((1, tm, d), lambda bi, i: (bi, i, 0)),
        out_shape=jax.ShapeDtypeStruct(h.shape, F32),
        scratch_shapes=[pltpu.VMEM((tm + 2 * HALO, d), BF16), pltpu.VMEM((tm, dff), BF16)],
        compiler_params=_params("arbitrary", "arbitrary"),
        name="conv_ffn",
    )(h, h, h, g.reshape(1, d), w_up, conv_w, conv_b.reshape(1, dff), w_down, gf)


def kernel(x, mem, mem_norm_g, mix_norm_g, w_in_ab, rpb, w_out_ab, w_in_c, conv_c, w_out_c,
           xa_norm_g, xa_wq, xa_wkv, xa_wo, ffn_norm_g, ffn_w_up, ffn_conv_w, ffn_conv_b,
           ffn_w_down, final_norm_g):
    b, s, d = x.shape
    depth = mix_norm_g.shape[0]
    n_mem = mem.shape[1]
    tm = min(SUB_TILE, s)
    tt = min(TOKEN_TILE, s)
    bf = lambda w: w.astype(BF16)

    cdft, m1, m3 = _dft_constants(s)
    mq, no = _mem_fold(mem.reshape(b * n_mem, d), mem_norm_g, bf(xa_wkv), bf(xa_wq), bf(xa_wo), b)

    h = x
    for layer in range(depth):
        j = layer // 2
        if layer % 2 == 0:
            g = mix_norm_g[layer]
            zt = _fourier_in(h, g, bf(w_in_ab[j, :, :FW]), cdft)
            qv, kt = _even_qkv(h, g, bf(w_in_ab[j, :, FW:]), tm)
            yf = _fft_mix(zt, m1, m3)
            h = _na_mix(qv, kt, rpb[j], yf, h, bf(w_out_ab[j]))
        else:
            h = _odd_mixer(h, mix_norm_g[layer], bf(w_in_c[j]), conv_c[j], bf(w_out_c[j]), tt)
        h = _xattn(h, xa_norm_g[layer], mq, no, layer, tt)
        h = _conv_ffn(h, ffn_norm_g[layer], bf(ffn_w_up[layer]), ffn_conv_w[layer], ffn_conv_b[layer],
                      bf(ffn_w_down[layer]), final_norm_g if layer == depth - 1 else None, tt)
    return h
```

```python
import functools
import math

import numpy as np
import jax
import jax.numpy as jnp
from jax import lax
from jax.experimental import pallas as pl
from jax.experimental.pallas import tpu as pltpu

EPS = 1e-6
GRID_W = 64
NA_KH, NA_KW = 8, 16
NA_HEADS = 4
NA_HEAD_DIM = 128
FOURIER_GROUPS = 4
FOURIER_GROUP_DIM = 128
XA_HEADS = 4
NEG_MASK = -1e30

LANES = 128
SUBLANES = 8
HALO = 16
SUB_TILE = 512
TOKEN_TILE = 1024
VMEM_LIMIT = 56 * 1024 * 1024

F32 = jnp.float32
BF16 = jnp.bfloat16

FW = FOURIER_GROUPS * FOURIER_GROUP_DIM
NW = NA_HEADS * NA_HEAD_DIM
NI = GRID_W


def _rms(x, g):
    ms = jnp.mean(x * x, axis=-1, keepdims=True)
    return (x * lax.rsqrt(ms + EPS)) * g


def _dot(a, b):
    return jnp.dot(a, b, preferred_element_type=F32)


def _dot_nt(a, b):
    return lax.dot_general(a, b, (((1,), (1,)), ((), ())), preferred_element_type=F32)


LOG2E = 1.4426950408889634


def _softmax_pv(s, v):
    m = jnp.max(s, axis=-1, keepdims=True)
    p = jnp.exp2(s - m)
    l = jnp.sum(p, axis=-1, keepdims=True)
    return _dot(p.astype(BF16), v) / l


def _const_spec(shape):
    nd = len(shape)
    return pl.BlockSpec(shape, lambda *_: (0,) * nd, pipeline_mode=pl.Buffered(1))


def _params(*sem):
    return pltpu.CompilerParams(dimension_semantics=sem, vmem_limit_bytes=VMEM_LIMIT)


def _strided_pitch(n):
    p = -(-n // SUBLANES) * SUBLANES
    return p if (p // SUBLANES) % 2 == 1 else p + SUBLANES


def _mem_fold_kernel(m_ref, g_ref, wkv_ref, wq_ref, wo_ref, mq_ref, no_ref):
    n_mem, d = m_ref.shape
    dh = d // XA_HEADS
    mn = _rms(m_ref[...], g_ref[...]).astype(BF16)
    kv = _dot(mn, wkv_ref[0]).astype(BF16)
    for hd in range(XA_HEADS):
        k = kv[:, hd * dh:(hd + 1) * dh]
        v = kv[:, d + hd * dh:d + (hd + 1) * dh]
        mq = _dot_nt(wq_ref[0, :, hd * dh:(hd + 1) * dh], k) * (LOG2E * dh ** -0.5)
        mq_ref[0, 0, :, hd * n_mem:(hd + 1) * n_mem] = mq.astype(BF16)
        no_ref[0, 0, hd * n_mem:(hd + 1) * n_mem, :] = _dot(v, wo_ref[0, hd * dh:(hd + 1) * dh, :]).astype(BF16)


def _mem_fold(mem2d, g, wkv, wq, wo, b):
    bm, d = mem2d.shape
    n_mem = bm // b
    nl = wkv.shape[0]
    hm = XA_HEADS * n_mem
    return pl.pallas_call(
        _mem_fold_kernel,
        grid=(nl, b),
        in_specs=[
            pl.BlockSpec((n_mem, d), lambda l, bi: (bi, 0)),
            pl.BlockSpec((1, d), lambda l, bi: (0, 0)),
            pl.BlockSpec((1, d, 2 * d), lambda l, bi: (l, 0, 0)),
            pl.BlockSpec((1, d, d), lambda l, bi: (l, 0, 0)),
            pl.BlockSpec((1, d, d), lambda l, bi: (l, 0, 0)),
        ],
        out_specs=[pl.BlockSpec((1, 1, d, hm), lambda l, bi: (l, bi, 0, 0)),
                   pl.BlockSpec((1, 1, hm, d), lambda l, bi: (l, bi, 0, 0))],
        out_shape=[jax.ShapeDtypeStruct((nl, b, d, hm), BF16), jax.ShapeDtypeStruct((nl, b, hm, d), BF16)],
        compiler_params=_params("arbitrary", "arbitrary"),
        name="mem_fold",
    )(mem2d, g.reshape(1, d), wkv, wq, wo)


def _even_qkv_kernel(x_ref, g_ref, w_ref, qkv_ref):
    hn = _rms(x_ref[...], g_ref[...]).astype(BF16)
    z = _dot(hn, w_ref[...])
    qkv_ref[:, :NW] = (z[:, :NW] * (LOG2E * NA_HEAD_DIM ** -0.5)).astype(BF16)
    qkv_ref[:, NW:] = z[:, NW:].astype(BF16)


def _even_qkv(h2d, g, w_qkv, tm):
    m, d = h2d.shape
    n = w_qkv.shape[1]
    return pl.pallas_call(
        _even_qkv_kernel,
        grid=(m // tm,),
        in_specs=[pl.BlockSpec((tm, d), lambda i: (i, 0)), _const_spec((1, d)), _const_spec((d, n))],
        out_specs=pl.BlockSpec((tm, n), lambda i: (i, 0)),
        out_shape=jax.ShapeDtypeStruct((m, n), BF16),
        compiler_params=_params("arbitrary"),
        name="even_qkv_proj",
    )(h2d, g.reshape(1, d), w_qkv)


FIN_GROUPS = 2


def _fourier_in_kernel(x_ref, g_ref, w_ref, cdft_ref, o_ref, sc_ref):
    no, d = x_ref.shape[1], x_ref.shape[3]
    gd = FOURIER_GROUP_DIM
    nslab = 2 * FOURIER_GROUPS
    cd = cdft_ref[...]
    for grp8 in range(FIN_GROUPS):
        i0 = grp8 * SUBLANES
        x = x_ref[0, :, i0:i0 + SUBLANES, :].reshape(no * SUBLANES, d)
        hn = _rms(x, g_ref[...]).astype(BF16)
        z = _dot(hn, w_ref[...])
        for grp in range(FOURIER_GROUPS):
            pq = _dot(z[:, grp * gd:(grp + 1) * gd].astype(BF16), cd)
            sc_ref[grp8, grp] = pq[:, :gd]
            sc_ref[grp8, FOURIER_GROUPS + grp] = pq[:, gd:]
        for il in range(SUBLANES):
            rows = [sc_ref[grp8, sl, pl.ds(il, no, stride=SUBLANES), :] for sl in range(nslab)]
            o_ref[0, i0 + il] = jnp.concatenate(rows, axis=1).astype(BF16)


def _fourier_in(h, g, w_f, cdft):
    b, s, d = h.shape
    no = s // NI
    h4 = h.reshape(b, no, NI, d)
    ni_step = FIN_GROUPS * SUBLANES
    return pl.pallas_call(
        _fourier_in_kernel,
        grid=(b, NI // ni_step),
        in_specs=[
            pl.BlockSpec((1, no, ni_step, d), lambda bi, i: (bi, 0, i, 0)),
            _const_spec((1, d)),
            _const_spec(w_f.shape),
            _const_spec(cdft.shape),
        ],
        out_specs=pl.BlockSpec((1, ni_step, no, 2 * FW), lambda bi, i: (bi, i, 0, 0)),
        out_shape=jax.ShapeDtypeStruct((b, NI, no, 2 * FW), BF16),
        scratch_shapes=[pltpu.VMEM((FIN_GROUPS, 2 * FW // LANES, no * SUBLANES, LANES), F32)],
        compiler_params=_params("arbitrary", "arbitrary"),
        name="fourier_in_proj",
    )(h4, g.reshape(1, d), w_f, cdft)


def _fft_kernel(zt_ref, m1_ref, m3_ref, o_ref, bt_ref, yf_ref, *, pitch_b, pitch_y):
    no = zt_ref.shape[2]
    nslab = FW // LANES
    j = pl.program_id(1)

    @pl.when(j == 0)
    def _():
        def stage1(i, carry):
            zi = zt_ref[0, i]
            rhs = jnp.concatenate([zi[:, :FW], zi[:, FW:]], axis=0)
            a = _dot(m1_ref[i], rhs)
            row0 = pl.multiple_of(i * pitch_b, SUBLANES)
            for sl in range(nslab):
                bt_ref[sl, pl.ds(row0, 2 * no), :] = a[:, sl * LANES:(sl + 1) * LANES]
            return carry

        lax.fori_loop(0, NI, stage1, 0, unroll=32)

        def stage2(k1, carry):
            br = [bt_ref[sl, pl.ds(k1, NI, stride=pitch_b), :] for sl in range(nslab)]
            bi = [bt_ref[sl, pl.ds(no + k1, NI, stride=pitch_b), :] for sl in range(nslab)]
            rhs = jnp.concatenate([jnp.concatenate(br, axis=1).astype(BF16),
                                   jnp.concatenate(bi, axis=1).astype(BF16)], axis=0)
            y = _dot(m3_ref[...], rhs)
            for sl in range(nslab):
                yf_ref[sl, pl.ds(k1, NI, stride=pitch_y), :] = y[:, sl * LANES:(sl + 1) * LANES]
            return carry

        lax.fori_loop(0, no, stage2, 0, unroll=16)

    for k2l in range(SUBLANES):
        row0 = pl.multiple_of((j * SUBLANES + k2l) * pitch_y, SUBLANES)
        rows = jnp.concatenate([yf_ref[sl, pl.ds(row0, no), :] for sl in range(nslab)], axis=1)
        o_ref[0, k2l * no:(k2l + 1) * no, :] = rows.astype(BF16)


def _fft_mix(zt, m1, m3):
    b, _, no, _ = zt.shape
    s = NI * no
    tm = SUBLANES * no
    pitch_b = _strided_pitch(2 * no)
    pitch_y = _strided_pitch(no)
    return pl.pallas_call(
        functools.partial(_fft_kernel, pitch_b=pitch_b, pitch_y=pitch_y),
        grid=(b, NI // SUBLANES),
        in_specs=[
            pl.BlockSpec((1, NI, no, 2 * FW), lambda bi, j: (jnp.minimum(bi + (j > 0), b - 1), 0, 0, 0)),
            _const_spec(m1.shape),
            _const_spec(m3.shape),
        ],
        out_specs=pl.BlockSpec((1, tm, FW), lambda bi, j: (bi, j, 0)),
        out_shape=jax.ShapeDtypeStruct((b, s, FW), BF16),
        scratch_shapes=[
            pltpu.VMEM((FW // LANES, NI * pitch_b, LANES), F32),
            pltpu.VMEM((FW // LANES, NI * pitch_y, LANES), F32),
        ],
        compiler_params=_params("arbitrary", "arbitrary"),
        name="fft_mix",
    )(zt, m1, m3)


def _dft_constants(s):
    n = FOURIER_GROUP_DIM
    jn = jnp.arange(n, dtype=jnp.int32)
    ang = ((jn[:, None] * jn[None, :]) % n).astype(F32) * (2.0 * math.pi / n)
    cdft = (jnp.concatenate([jnp.cos(ang), jnp.sin(ang)], axis=1) * (n ** -0.5)).astype(BF16)
    no = s // NI
    i = jnp.arange(NI, dtype=jnp.int32)[:, None, None]
    k1 = jnp.arange(no, dtype=jnp.int32)[None, :, None]
    o = jnp.arange(no, dtype=jnp.int32)[None, None, :]
    ang1 = ((k1 * (NI * o + i)) % s).astype(F32) * (2.0 * math.pi / s)
    c1, s1 = jnp.cos(ang1), jnp.sin(ang1)
    m1 = jnp.concatenate([jnp.concatenate([c1, -s1], axis=2),
                          jnp.concatenate([-s1, -c1], axis=2)], axis=1).astype(BF16)
    ji = jnp.arange(NI, dtype=jnp.int32)
    ang3 = ((ji[:, None] * ji[None, :]) % NI).astype(F32) * (2.0 * math.pi / NI)
    m3 = (jnp.concatenate([jnp.cos(ang3), jnp.sin(ang3)], axis=1) * (s ** -0.5)).astype(BF16)
    return cdft, m1, m3


NA_QROWS = 8


def _na_kernel(q_ref, k_ref, v_ref, bias_ref, yf_ref, x_ref, w_ref, o_ref, ya_ref, s_ref, *, rows):
    blk = pl.program_id(1)
    nk = NA_KH * GRID_W
    dh = NA_HEAD_DIM
    kstarts, deltas = [], []
    for a in range(NA_QROWS):
        i = blk * NA_QROWS + a
        rs = jnp.clip(i - NA_KH // 2, 0, rows - NA_KH)
        kstarts.append(pl.multiple_of(rs * GRID_W, GRID_W))
        deltas.append(i - rs)
    for a in range(NA_QROWS):
        for hd in range(NA_HEADS):
            q = q_ref[0, a * GRID_W:(a + 1) * GRID_W, hd * dh:(hd + 1) * dh]
            k = k_ref[0, pl.ds(kstarts[a], nk), hd * dh:(hd + 1) * dh]
            s_ref[a * NA_HEADS + hd] = _dot_nt(q, k) + bias_ref[deltas[a], hd]
    for a in range(NA_QROWS):
        for hd in range(NA_HEADS):
            v = v_ref[0, pl.ds(kstarts[a], nk), hd * dh:(hd + 1) * dh]
            ya_ref[a * GRID_W:(a + 1) * GRID_W, hd * dh:(hd + 1) * dh] = _softmax_pv(
                s_ref[a * NA_HEADS + hd], v).astype(BF16)
    o_ref[0] = x_ref[0] + _dot(yf_ref[0], w_ref[:FW, :]) + _dot(ya_ref[...], w_ref[FW:, :])


def _na_bias_table(rpb):
    cols = np.arange(GRID_W)
    cs = np.clip(cols - NA_KW // 2, 0, GRID_W - NA_KW)
    col_valid = (cols[None, :] >= cs[:, None]) & (cols[None, :] < cs[:, None] + NA_KW)
    col_off = np.clip(cols[None, :] - cols[:, None] + NA_KW - 1, 0, 2 * NA_KW - 2)
    delta = np.arange(NA_KH)
    row_off = np.arange(NA_KH)[None, :] - delta[:, None] + NA_KH - 1
    sel_r = (row_off[:, :, None] == np.arange(2 * NA_KH - 1)).astype(np.float32)
    sel_c = (col_off[:, :, None] == np.arange(2 * NA_KW - 1)).astype(np.float32)
    bias = jnp.einsum("hok,dro,jck->dhjrc", rpb, sel_r, sel_c, precision=lax.Precision.HIGHEST)
    bias = jnp.where(col_valid[None, None, :, None, :], bias * LOG2E, NEG_MASK)
    return bias.reshape(NA_KH, rpb.shape[0], GRID_W, NA_KH * GRID_W).astype(F32)


def _na_mix(qkv, rpb, yf, h, w_out):
    b, s, d = h.shape
    rows = s // GRID_W
    assert rows >= NA_KH and rows % NA_QROWS == 0
    table = _na_bias_table(rpb)
    tq = NA_QROWS * GRID_W
    nk = NA_KH * GRID_W
    return pl.pallas_call(
        functools.partial(_na_kernel, rows=rows),
        grid=(b, rows // NA_QROWS),
        in_specs=[
            pl.BlockSpec((1, tq, NW), lambda bi, i: (bi, i, 0)),
            pl.BlockSpec((1, s, NW), lambda bi, i: (bi, 0, 1)),
            pl.BlockSpec((1, s, NW), lambda bi, i: (bi, 0, 2)),
            _const_spec(table.shape),
            pl.BlockSpec((1, tq, FW), lambda bi, i: (bi, i, 0)),
            pl.BlockSpec((1, tq, d), lambda bi, i: (bi, i, 0)),
            _const_spec(w_out.shape),
        ],
        out_specs=pl.BlockSpec((1, tq, d), lambda bi, i: (bi, i, 0)),
        out_shape=jax.ShapeDtypeStruct(h.shape, F32),
        scratch_shapes=[pltpu.VMEM((tq, NW), BF16), pltpu.VMEM((NA_QROWS * NA_HEADS, GRID_W, nk), F32)],
        compiler_params=_params("arbitrary", "arbitrary"),
        name="neighborhood_attn",
    )(qkv, qkv, qkv, table, yf, h, w_out)


def _halo_specs(tm, s, d):
    r = tm // HALO
    last_blk = s // HALO - 1

    def prev_map(b, i):
        return (b, jnp.maximum(i * r - 1, 0), 0)

    def next_map(b, i):
        return (b, jnp.minimum((i + 1) * r, last_blk), 0)

    return [
        pl.BlockSpec((1, tm, d), lambda b, i: (b, i, 0)),
        pl.BlockSpec((1, HALO, d), prev_map),
        pl.BlockSpec((1, HALO, d), next_map),
    ]


def _fill_normed_ext(hn_ref, x, xp, xn, g):
    tm = x.shape[0]
    i = pl.program_id(1)
    first = i == 0
    last = i == pl.num_programs(1) - 1
    hp = jnp.where(first, 0.0, _rms(xp, g))
    hx = jnp.where(last, 0.0, _rms(xn, g))
    hn_ref[:HALO, :] = hp.astype(BF16)
    hn_ref[HALO:HALO + tm, :] = _rms(x, g).astype(BF16)
    hn_ref[HALO + tm:, :] = hx.astype(BF16)


def _conv3_rows(z, cw, tm):
    n = z.shape[0]
    prev = pltpu.roll(z, 1, 0)[HALO:HALO + tm]
    nxt = pltpu.roll(z, n - 1, 0)[HALO:HALO + tm]
    return prev * cw[0:1, :] + z[HALO:HALO + tm] * cw[1:2, :] + nxt * cw[2:3, :]


def _odd_mixer_kernel(x_ref, xp_ref, xn_ref, g_ref, win_ref, cw_ref, wout_ref, o_ref, hn_ref, y_ref, *, nc):
    tm, d = x_ref.shape[1], x_ref.shape[2]
    sub = min(SUB_TILE, tm)
    _fill_normed_ext(hn_ref, x_ref[0], xp_ref[0], xn_ref[0], g_ref[...])
    nchunks = d // nc
    for t in range(tm // sub):
        r0 = t * sub
        hn_ext = hn_ref[r0:r0 + sub + 2 * HALO, :]
        hn = hn_ref[HALO + r0:HALO + r0 + sub, :]

        def project(c):
            lo, hi = c * nc, (c + 1) * nc
            return (_dot(hn, win_ref[:, lo:hi]),
                    _dot(hn_ext, win_ref[:, d + lo:d + hi]),
                    _dot(hn_ext, win_ref[:, 2 * d + lo:2 * d + hi]))

        cur = project(0)
        for c in range(nchunks):
            nxt = project(c + 1) if c + 1 < nchunks else None
            gate_b, gate_c, u = cur
            lo, hi = c * nc, (c + 1) * nc
            y_ref[r0:r0 + sub, lo:hi] = (gate_b * _conv3_rows(gate_c * u, cw_ref[:, lo:hi], sub)).astype(BF16)
            cur = nxt
        o_ref[0, r0:r0 + sub, :] = x_ref[0, r0:r0 + sub, :] + _dot(y_ref[r0:r0 + sub, :], wout_ref[...])


def _odd_mixer(h, g, w_in, conv_w, w_out, tm, nc=256):
    b, s, d = h.shape
    return pl.pallas_call(
        functools.partial(_odd_mixer_kernel, nc=nc),
        grid=(b, s // tm),
        in_specs=_halo_specs(tm, s, d) + [
            _const_spec((1, d)),
            _const_spec(w_in.shape),
            _const_spec(conv_w.shape),
            _const_spec(w_out.shape),
        ],
        out_specs=pl.BlockSpec((1, tm, d), lambda bi, i: (bi, i, 0)),
        out_shape=jax.ShapeDtypeStruct(h.shape, F32),
        scratch_shapes=[pltpu.VMEM((tm + 2 * HALO, d), BF16), pltpu.VMEM((tm, d), BF16)],
        compiler_params=_params("arbitrary", "arbitrary"),
        name="odd_mixer",
    )(h, h, h, g.reshape(1, d), w_in, conv_w, w_out)


def _xattn_kernel(x_ref, g_ref, mq_ref, no_ref, o_ref, p_ref, s_ref):
    n_mem = mq_ref.shape[3] // XA_HEADS
    tm = x_ref.shape[1]
    sub = min(SUB_TILE, tm)
    for t in range(tm // sub):
        r0 = t * sub
        x = x_ref[0, r0:r0 + sub, :]
        hn = _rms(x, g_ref[...]).astype(BF16)
        s_ref[r0:r0 + sub, :] = _dot(hn, mq_ref[0, 0])
        for hd in range(XA_HEADS):
            s = s_ref[r0:r0 + sub, hd * n_mem:(hd + 1) * n_mem]
            p = jnp.exp2(s - jnp.max(s, axis=-1, keepdims=True))
            r = 1.0 / jnp.sum(p, axis=-1, keepdims=True)
            p_ref[r0:r0 + sub, hd * n_mem:(hd + 1) * n_mem] = (p * r).astype(BF16)
        o_ref[0, r0:r0 + sub, :] = x + _dot(p_ref[r0:r0 + sub, :], no_ref[0, 0])


def _xattn(h, g, mq, no, layer, tm):
    b, s, d = h.shape
    hm = mq.shape[3]
    return pl.pallas_call(
        _xattn_kernel,
        grid=(b, s // tm),
        in_specs=[
            pl.BlockSpec((1, tm, d), lambda bi, i: (bi, i, 0)),
            _const_spec((1, d)),
            pl.BlockSpec((1, 1, d, hm), lambda bi, i: (layer, bi, 0, 0)),
            pl.BlockSpec((1, 1, hm, d), lambda bi, i: (layer, bi, 0, 0)),
        ],
        out_specs=pl.BlockSpec((1, tm, d), lambda bi, i: (bi, i, 0)),
        out_shape=jax.ShapeDtypeStruct(h.shape, F32),
        scratch_shapes=[pltpu.VMEM((tm, hm), BF16), pltpu.VMEM((tm, hm), F32)],
        compiler_params=_params("arbitrary", "arbitrary"),
        name="mem_xattn",
    )(h, g.reshape(1, d), mq, no)


def _gelu_exact(x):
    return 0.5 * x * (1.0 + lax.erf(x * (1.0 / math.sqrt(2.0))))


def _ffn_kernel(x_ref, xp_ref, xn_ref, g_ref, wup_ref, cw_ref, cb_ref, wdn_ref, gf_ref, o_ref,
                hn_ref, a_ref, *, nc, final_norm):
    tm = x_ref.shape[1]
    dff = wdn_ref.shape[0]
    sub = min(SUB_TILE, tm)
    _fill_normed_ext(hn_ref, x_ref[0], xp_ref[0], xn_ref[0], g_ref[...])
    nchunks = dff // nc
    for t in range(tm // sub):
        r0 = t * sub
        hn_ext = hn_ref[r0:r0 + sub + 2 * HALO, :]
        hn = hn_ref[HALO + r0:HALO + r0 + sub, :]

        def project(c):
            lo, hi = c * nc, (c + 1) * nc
            return _dot(hn, wup_ref[:, lo:hi]), _dot(hn_ext, wup_ref[:, dff + lo:dff + hi])

        cur = project(0)
        for c in range(nchunks):
            nxt = project(c + 1) if c + 1 < nchunks else None
            u, gp = cur
            lo, hi = c * nc, (c + 1) * nc
            gate = _conv3_rows(gp, cw_ref[:, lo:hi], sub) + cb_ref[:, lo:hi]
            a_ref[r0:r0 + sub, lo:hi] = (_gelu_exact(gate) * u).astype(BF16)
            cur = nxt
        acc = x_ref[0, r0:r0 + sub, :] + _dot(a_ref[r0:r0 + sub, :], wdn_ref[...])
        if final_norm:
            acc = _rms(acc, gf_ref[...])
        o_ref[0, r0:r0 + sub, :] = acc


def _conv_ffn(h, g, w_up, conv_w, conv_b, w_down, final_g, tm, nc=256):
    b, s, d = h.shape
    dff = w_down.shape[0]
    final_norm = final_g is not None
    gf = (final_g if final_norm else g).reshape(1, d)
    return pl.pallas_call(
        functools.partial(_ffn_kernel, nc=nc, final_norm=final_norm),
        grid=(b, s // tm),
        in_specs=_halo_specs(tm, s, d) + [
            _const_spec((1, d)),
            _const_spec(w_up.shape),
            _const_spec(conv_w.shape),
            _const_spec((1, dff)),
            _const_spec(w_down.shape),
            _const_spec((1, d)),
        ],
        out_specs=pl.BlockSpec((1, tm, d), lambda bi, i: (bi, i, 0)),
        out_shape=jax.ShapeDtypeStruct(h.shape, F32),
        scratch_shapes=[pltpu.VMEM((tm + 2 * HALO, d), BF16), pltpu.VMEM((tm, dff), BF16)],
        compiler_params=_params("arbitrary", "arbitrary"),
        name="conv_ffn",
    )(h, h, h, g.reshape(1, d), w_up, conv_w, conv_b.reshape(1, dff), w_down, gf)


def kernel(x, mem, mem_norm_g, mix_norm_g, w_in_ab, rpb, w_out_ab, w_in_c, conv_c, w_out_c,
           xa_norm_g, xa_wq, xa_wkv, xa_wo, ffn_norm_g, ffn_w_up, ffn_conv_w, ffn_conv_b,
           ffn_w_down, final_norm_g):
    b, s, d = x.shape
    depth = mix_norm_g.shape[0]
    n_mem = mem.shape[1]
    tm = min(SUB_TILE, s)
    tt = min(TOKEN_TILE, s)
    bf = lambda w: w.astype(BF16)

    cdft, m1, m3 = _dft_constants(s)
    mq, no = _mem_fold(mem.reshape(b * n_mem, d), mem_norm_g, bf(xa_wkv), bf(xa_wq), bf(xa_wo), b)

    h = x
    for layer in range(depth):
        j = layer // 2
        if layer % 2 == 0:
            g = mix_norm_g[layer]
            zt = _fourier_in(h, g, bf(w_in_ab[j, :, :FW]), cdft)
            qkv = _even_qkv(h.reshape(b * s, d), g, bf(w_in_ab[j, :, FW:]), tm).reshape(b, s, -1)
            yf = _fft_mix(zt, m1, m3)
            h = _na_mix(qkv, rpb[j], yf, h, bf(w_out_ab[j]))
        else:
            h = _odd_mixer(h, mix_norm_g[layer], bf(w_in_c[j]), conv_c[j], bf(w_out_c[j]), tt)
        h = _xattn(h, xa_norm_g[layer], mq, no, layer, tt)
        h = _conv_ffn(h, ffn_norm_g[layer], bf(ffn_w_up[layer]), ffn_conv_w[layer], ffn_conv_b[layer],
                      bf(ffn_w_down[layer]), final_norm_g if layer == depth - 1 else None, tt)
    return h
```

```python
import functools
import math

import numpy as np
import jax
import jax.numpy as jnp
from jax import lax
from jax.experimental import pallas as pl
from jax.experimental.pallas import tpu as pltpu

EPS = 1e-6
GRID_W = 64
NA_KH, NA_KW = 8, 16
NA_HEADS = 4
NA_HEAD_DIM = 128
FOURIER_GROUPS = 4
FOURIER_GROUP_DIM = 128
XA_HEADS = 4
NEG_MASK = -1e30

LANES = 128
SUBLANES = 8
HALO = 16
SUB_TILE = 512
TOKEN_TILE = 1024
VMEM_LIMIT = 56 * 1024 * 1024

F32 = jnp.float32
BF16 = jnp.bfloat16

FW = FOURIER_GROUPS * FOURIER_GROUP_DIM
NW = NA_HEADS * NA_HEAD_DIM
NI = GRID_W


def _rms(x, g):
    ms = jnp.mean(x * x, axis=-1, keepdims=True)
    return (x * lax.rsqrt(ms + EPS)) * g


def _dot(a, b):
    return jnp.dot(a, b, preferred_element_type=F32)


def _dot_nt(a, b):
    return lax.dot_general(a, b, (((1,), (1,)), ((), ())), preferred_element_type=F32)


LOG2E = 1.4426950408889634


def _softmax_pv(s, v):
    m = jnp.max(s, axis=-1, keepdims=True)
    p = jnp.exp2(s - m)
    l = jnp.sum(p, axis=-1, keepdims=True)
    return _dot(p.astype(BF16), v) / l


def _const_spec(shape):
    nd = len(shape)
    return pl.BlockSpec(shape, lambda *_: (0,) * nd, pipeline_mode=pl.Buffered(1))


def _params(*sem):
    return pltpu.CompilerParams(dimension_semantics=sem, vmem_limit_bytes=VMEM_LIMIT)


def _strided_pitch(n):
    p = -(-n // SUBLANES) * SUBLANES
    return p if (p // SUBLANES) % 2 == 1 else p + SUBLANES


def _mem_fold_kernel(m_ref, g_ref, wkv_ref, wq_ref, wo_ref, mq_ref, no_ref):
    n_mem, d = m_ref.shape
    dh = d // XA_HEADS
    mn = _rms(m_ref[...], g_ref[...]).astype(BF16)
    kv = _dot(mn, wkv_ref[0]).astype(BF16)
    for hd in range(XA_HEADS):
        k = kv[:, hd * dh:(hd + 1) * dh]
        v = kv[:, d + hd * dh:d + (hd + 1) * dh]
        mq = _dot_nt(wq_ref[0, :, hd * dh:(hd + 1) * dh], k) * (LOG2E * dh ** -0.5)
        mq_ref[0, 0, :, hd * n_mem:(hd + 1) * n_mem] = mq.astype(BF16)
        no_ref[0, 0, hd * n_mem:(hd + 1) * n_mem, :] = _dot(v, wo_ref[0, hd * dh:(hd + 1) * dh, :]).astype(BF16)


def _mem_fold(mem2d, g, wkv, wq, wo, b):
    bm, d = mem2d.shape
    n_mem = bm // b
    nl = wkv.shape[0]
    hm = XA_HEADS * n_mem
    return pl.pallas_call(
        _mem_fold_kernel,
        grid=(nl, b),
        in_specs=[
            pl.BlockSpec((n_mem, d), lambda l, bi: (bi, 0)),
            pl.BlockSpec((1, d), lambda l, bi: (0, 0)),
            pl.BlockSpec((1, d, 2 * d), lambda l, bi: (l, 0, 0)),
            pl.BlockSpec((1, d, d), lambda l, bi: (l, 0, 0)),
            pl.BlockSpec((1, d, d), lambda l, bi: (l, 0, 0)),
        ],
        out_specs=[pl.BlockSpec((1, 1, d, hm), lambda l, bi: (l, bi, 0, 0)),
                   pl.BlockSpec((1, 1, hm, d), lambda l, bi: (l, bi, 0, 0))],
        out_shape=[jax.ShapeDtypeStruct((nl, b, d, hm), BF16), jax.ShapeDtypeStruct((nl, b, hm, d), BF16)],
        compiler_params=_params("arbitrary", "arbitrary"),
        name="mem_fold",
    )(mem2d, g.reshape(1, d), wkv, wq, wo)


def _even_qkv_kernel(x_ref, g_ref, w_ref, qkv_ref):
    tm = x_ref.shape[0]
    sub = min(SUB_TILE, tm)
    for t in range(tm // sub):
        r0 = t * sub
        hn = _rms(x_ref[r0:r0 + sub, :], g_ref[...]).astype(BF16)
        z = _dot(hn, w_ref[...])
        qkv_ref[r0:r0 + sub, :NW] = (z[:, :NW] * (LOG2E * NA_HEAD_DIM ** -0.5)).astype(BF16)
        qkv_ref[r0:r0 + sub, NW:] = z[:, NW:].astype(BF16)


def _even_qkv(h2d, g, w_qkv, tm):
    m, d = h2d.shape
    n = w_qkv.shape[1]
    return pl.pallas_call(
        _even_qkv_kernel,
        grid=(m // tm,),
        in_specs=[pl.BlockSpec((tm, d), lambda i: (i, 0)), _const_spec((1, d)), _const_spec((d, n))],
        out_specs=pl.BlockSpec((tm, n), lambda i: (i, 0)),
        out_shape=jax.ShapeDtypeStruct((m, n), BF16),
        compiler_params=_params("arbitrary"),
        name="even_qkv_proj",
    )(h2d, g.reshape(1, d), w_qkv)


FIN_GROUPS = 2


def _fourier_in_kernel(x_ref, g_ref, w_ref, cdft_ref, o_ref, sc_ref):
    no, d = x_ref.shape[1], x_ref.shape[3]
    gd = FOURIER_GROUP_DIM
    nslab = 2 * FOURIER_GROUPS
    cd = cdft_ref[...]
    for grp8 in range(FIN_GROUPS):
        i0 = grp8 * SUBLANES
        x = x_ref[0, :, i0:i0 + SUBLANES, :].reshape(no * SUBLANES, d)
        hn = _rms(x, g_ref[...]).astype(BF16)
        z = _dot(hn, w_ref[...])
        for grp in range(FOURIER_GROUPS):
            pq = _dot(z[:, grp * gd:(grp + 1) * gd].astype(BF16), cd)
            sc_ref[grp8, grp] = pq[:, :gd]
            sc_ref[grp8, FOURIER_GROUPS + grp] = pq[:, gd:]
        for il in range(SUBLANES):
            rows = [sc_ref[grp8, sl, pl.ds(il, no, stride=SUBLANES), :] for sl in range(nslab)]
            o_ref[0, i0 + il] = jnp.concatenate(rows, axis=1).astype(BF16)


def _fourier_in(h, g, w_f, cdft):
    b, s, d = h.shape
    no = s // NI
    h4 = h.reshape(b, no, NI, d)
    ni_step = FIN_GROUPS * SUBLANES
    return pl.pallas_call(
        _fourier_in_kernel,
        grid=(b, NI // ni_step),
        in_specs=[
            pl.BlockSpec((1, no, ni_step, d), lambda bi, i: (bi, 0, i, 0)),
            _const_spec((1, d)),
            _const_spec(w_f.shape),
            _const_spec(cdft.shape),
        ],
        out_specs=pl.BlockSpec((1, ni_step, no, 2 * FW), lambda bi, i: (bi, i, 0, 0)),
        out_shape=jax.ShapeDtypeStruct((b, NI, no, 2 * FW), BF16),
        scratch_shapes=[pltpu.VMEM((FIN_GROUPS, 2 * FW // LANES, no * SUBLANES, LANES), F32)],
        compiler_params=_params("arbitrary", "arbitrary"),
        name="fourier_in_proj",
    )(h4, g.reshape(1, d), w_f, cdft)


FFT_OUT_STEPS = 2


def _fft_kernel(zt_ref, m1_ref, m3_ref, o_ref, bt_ref, yf_ref, *, pitch_b, pitch_y):
    no = zt_ref.shape[2]
    nslab = FW // LANES
    j = pl.program_id(1)

    @pl.when(j == 0)
    def _():
        def stage1(i, carry):
            zi = zt_ref[0, i]
            rhs = jnp.concatenate([zi[:, :FW], zi[:, FW:]], axis=0)
            a = _dot(m1_ref[i], rhs)
            row0 = pl.multiple_of(i * pitch_b, SUBLANES)
            for sl in range(nslab):
                bt_ref[sl, pl.ds(row0, 2 * no), :] = a[:, sl * LANES:(sl + 1) * LANES]
            return carry

        lax.fori_loop(0, NI, stage1, 0, unroll=32)

        def stage2(k1, carry):
            br = [bt_ref[sl, pl.ds(k1, NI, stride=pitch_b), :] for sl in range(nslab)]
            bi = [bt_ref[sl, pl.ds(no + k1, NI, stride=pitch_b), :] for sl in range(nslab)]
            rhs = jnp.concatenate([jnp.concatenate(br, axis=1).astype(BF16),
                                   jnp.concatenate(bi, axis=1).astype(BF16)], axis=0)
            y = _dot(m3_ref[...], rhs)
            for sl in range(nslab):
                yf_ref[sl, pl.ds(k1, NI, stride=pitch_y), :] = y[:, sl * LANES:(sl + 1) * LANES]
            return carry

        lax.fori_loop(0, no, stage2, 0, unroll=16)

    k2_step = NI // FFT_OUT_STEPS
    for k2l in range(k2_step):
        row0 = pl.multiple_of((j * k2_step + k2l) * pitch_y, SUBLANES)
        rows = jnp.concatenate([yf_ref[sl, pl.ds(row0, no), :] for sl in range(nslab)], axis=1)
        o_ref[0, k2l * no:(k2l + 1) * no, :] = rows.astype(BF16)


def _fft_mix(zt, m1, m3):
    b, _, no, _ = zt.shape
    s = NI * no
    tm = s // FFT_OUT_STEPS
    pitch_b = _strided_pitch(2 * no)
    pitch_y = _strided_pitch(no)
    return pl.pallas_call(
        functools.partial(_fft_kernel, pitch_b=pitch_b, pitch_y=pitch_y),
        grid=(b, FFT_OUT_STEPS),
        in_specs=[
            pl.BlockSpec((1, NI, no, 2 * FW), lambda bi, j: (jnp.minimum(bi + (j > 0), b - 1), 0, 0, 0)),
            _const_spec(m1.shape),
            _const_spec(m3.shape),
        ],
        out_specs=pl.BlockSpec((1, tm, FW), lambda bi, j: (bi, j, 0)),
        out_shape=jax.ShapeDtypeStruct((b, s, FW), BF16),
        scratch_shapes=[
            pltpu.VMEM((FW // LANES, NI * pitch_b, LANES), F32),
            pltpu.VMEM((FW // LANES, NI * pitch_y, LANES), F32),
        ],
        compiler_params=_params("arbitrary", "arbitrary"),
        name="fft_mix",
    )(zt, m1, m3)


def _dft_constants(s):
    n = FOURIER_GROUP_DIM
    jn = jnp.arange(n, dtype=jnp.int32)
    ang = ((jn[:, None] * jn[None, :]) % n).astype(F32) * (2.0 * math.pi / n)
    cdft = (jnp.concatenate([jnp.cos(ang), jnp.sin(ang)], axis=1) * (n ** -0.5)).astype(BF16)
    no = s // NI
    i = jnp.arange(NI, dtype=jnp.int32)[:, None, None]
    k1 = jnp.arange(no, dtype=jnp.int32)[None, :, None]
    o = jnp.arange(no, dtype=jnp.int32)[None, None, :]
    ang1 = ((k1 * (NI * o + i)) % s).astype(F32) * (2.0 * math.pi / s)
    c1, s1 = jnp.cos(ang1), jnp.sin(ang1)
    m1 = jnp.concatenate([jnp.concatenate([c1, -s1], axis=2),
                          jnp.concatenate([-s1, -c1], axis=2)], axis=1).astype(BF16)
    ji = jnp.arange(NI, dtype=jnp.int32)
    ang3 = ((ji[:, None] * ji[None, :]) % NI).astype(F32) * (2.0 * math.pi / NI)
    m3 = (jnp.concatenate([jnp.cos(ang3), jnp.sin(ang3)], axis=1) * (s ** -0.5)).astype(BF16)
    return cdft, m1, m3


NA_QROWS = 16


def _na_kernel(q_ref, k_ref, v_ref, bias_ref, yf_ref, x_ref, w_ref, o_ref, ya_ref, s_ref, *, rows):
    blk = pl.program_id(1)
    nk = NA_KH * GRID_W
    dh = NA_HEAD_DIM
    kstarts, deltas = [], []
    for a in range(NA_QROWS):
        i = blk * NA_QROWS + a
        rs = jnp.clip(i - NA_KH // 2, 0, rows - NA_KH)
        kstarts.append(pl.multiple_of(rs * GRID_W, GRID_W))
        deltas.append(i - rs)
    for a in range(NA_QROWS):
        for hd in range(NA_HEADS):
            q = q_ref[0, a * GRID_W:(a + 1) * GRID_W, hd * dh:(hd + 1) * dh]
            k = k_ref[0, pl.ds(kstarts[a], nk), hd * dh:(hd + 1) * dh]
            s_ref[a * NA_HEADS + hd] = _dot_nt(q, k) + bias_ref[deltas[a], hd]
    for a in range(NA_QROWS):
        for hd in range(NA_HEADS):
            v = v_ref[0, pl.ds(kstarts[a], nk), hd * dh:(hd + 1) * dh]
            ya_ref[a * GRID_W:(a + 1) * GRID_W, hd * dh:(hd + 1) * dh] = _softmax_pv(
                s_ref[a * NA_HEADS + hd], v).astype(BF16)
    o_ref[0] = x_ref[0] + _dot(yf_ref[0], w_ref[:FW, :]) + _dot(ya_ref[...], w_ref[FW:, :])


def _na_bias_table(rpb):
    cols = np.arange(GRID_W)
    cs = np.clip(cols - NA_KW // 2, 0, GRID_W - NA_KW)
    col_valid = (cols[None, :] >= cs[:, None]) & (cols[None, :] < cs[:, None] + NA_KW)
    col_off = np.clip(cols[None, :] - cols[:, None] + NA_KW - 1, 0, 2 * NA_KW - 2)
    delta = np.arange(NA_KH)
    row_off = np.arange(NA_KH)[None, :] - delta[:, None] + NA_KH - 1
    sel_r = (row_off[:, :, None] == np.arange(2 * NA_KH - 1)).astype(np.float32)
    sel_c = (col_off[:, :, None] == np.arange(2 * NA_KW - 1)).astype(np.float32)
    bias = jnp.einsum("hok,dro,jck->dhjrc", rpb, sel_r, sel_c, precision=lax.Precision.HIGHEST)
    bias = jnp.where(col_valid[None, None, :, None, :], bias * LOG2E, NEG_MASK)
    return bias.reshape(NA_KH, rpb.shape[0], GRID_W, NA_KH * GRID_W).astype(F32)


def _na_mix(qkv, rpb, yf, h, w_out):
    b, s, d = h.shape
    rows = s // GRID_W
    assert rows >= NA_KH and rows % NA_QROWS == 0
    table = _na_bias_table(rpb)
    tq = NA_QROWS * GRID_W
    nk = NA_KH * GRID_W
    return pl.pallas_call(
        functools.partial(_na_kernel, rows=rows),
        grid=(b, rows // NA_QROWS),
        in_specs=[
            pl.BlockSpec((1, tq, NW), lambda bi, i: (bi, i, 0)),
            pl.BlockSpec((1, s, NW), lambda bi, i: (bi, 0, 1)),
            pl.BlockSpec((1, s, NW), lambda bi, i: (bi, 0, 2)),
            _const_spec(table.shape),
            pl.BlockSpec((1, tq, FW), lambda bi, i: (bi, i, 0)),
            pl.BlockSpec((1, tq, d), lambda bi, i: (bi, i, 0)),
            _const_spec(w_out.shape),
        ],
        out_specs=pl.BlockSpec((1, tq, d), lambda bi, i: (bi, i, 0)),
        out_shape=jax.ShapeDtypeStruct(h.shape, F32),
        scratch_shapes=[pltpu.VMEM((tq, NW), BF16), pltpu.VMEM((NA_QROWS * NA_HEADS, GRID_W, nk), F32)],
        compiler_params=_params("arbitrary", "arbitrary"),
        name="neighborhood_attn",
    )(qkv, qkv, qkv, table, yf, h, w_out)


def _halo_specs(tm, s, d):
    r = tm // HALO
    last_blk = s // HALO - 1

    def prev_map(b, i):
        return (b, jnp.maximum(i * r - 1, 0), 0)

    def next_map(b, i):
        return (b, jnp.minimum((i + 1) * r, last_blk), 0)

    return [
        pl.BlockSpec((1, tm, d), lambda b, i: (b, i, 0)),
        pl.BlockSpec((1, HALO, d), prev_map),
        pl.BlockSpec((1, HALO, d), next_map),
    ]


def _fill_normed_ext(hn_ref, x, xp, xn, g):
    tm = x.shape[0]
    i = pl.program_id(1)
    first = i == 0
    last = i == pl.num_programs(1) - 1
    hp = jnp.where(first, 0.0, _rms(xp, g))
    hx = jnp.where(last, 0.0, _rms(xn, g))
    hn_ref[:HALO, :] = hp.astype(BF16)
    hn_ref[HALO:HALO + tm, :] = _rms(x, g).astype(BF16)
    hn_ref[HALO + tm:, :] = hx.astype(BF16)


def _conv3_rows(z, cw, tm):
    n = z.shape[0]
    prev = pltpu.roll(z, 1, 0)[HALO:HALO + tm]
    nxt = pltpu.roll(z, n - 1, 0)[HALO:HALO + tm]
    return prev * cw[0:1, :] + z[HALO:HALO + tm] * cw[1:2, :] + nxt * cw[2:3, :]


def _odd_mixer_kernel(x_ref, xp_ref, xn_ref, g_ref, win_ref, cw_ref, wout_ref, o_ref, hn_ref, y_ref, *, nc):
    tm, d = x_ref.shape[1], x_ref.shape[2]
    sub = min(SUB_TILE, tm)
    _fill_normed_ext(hn_ref, x_ref[0], xp_ref[0], xn_ref[0], g_ref[...])
    nchunks = d // nc
    for t in range(tm // sub):
        r0 = t * sub
        hn_ext = hn_ref[r0:r0 + sub + 2 * HALO, :]
        hn = hn_ref[HALO + r0:HALO + r0 + sub, :]

        def project(c):
            lo, hi = c * nc, (c + 1) * nc
            return (_dot(hn, win_ref[:, lo:hi]),
                    _dot(hn_ext, win_ref[:, d + lo:d + hi]),
                    _dot(hn_ext, win_ref[:, 2 * d + lo:2 * d + hi]))

        cur = project(0)
        for c in range(nchunks):
            nxt = project(c + 1) if c + 1 < nchunks else None
            gate_b, gate_c, u = cur
            lo, hi = c * nc, (c + 1) * nc
            y_ref[r0:r0 + sub, lo:hi] = (gate_b * _conv3_rows(gate_c * u, cw_ref[:, lo:hi], sub)).astype(BF16)
            cur = nxt
        o_ref[0, r0:r0 + sub, :] = x_ref[0, r0:r0 + sub, :] + _dot(y_ref[r0:r0 + sub, :], wout_ref[...])


def _odd_mixer(h, g, w_in, conv_w, w_out, tm, nc=256):
    b, s, d = h.shape
    return pl.pallas_call(
        functools.partial(_odd_mixer_kernel, nc=nc),
        grid=(b, s // tm),
        in_specs=_halo_specs(tm, s, d) + [
            _const_spec((1, d)),
            _const_spec(w_in.shape),
            _const_spec(conv_w.shape),
            _const_spec(w_out.shape),
        ],
        out_specs=pl.BlockSpec((1, tm, d), lambda bi, i: (bi, i, 0)),
        out_shape=jax.ShapeDtypeStruct(h.shape, F32),
        scratch_shapes=[pltpu.VMEM((tm + 2 * HALO, d), BF16), pltpu.VMEM((tm, d), BF16)],
        compiler_params=_params("arbitrary", "arbitrary"),
        name="odd_mixer",
    )(h, h, h, g.reshape(1, d), w_in, conv_w, w_out)


def _xattn_kernel(x_ref, g_ref, mq_ref, no_ref, o_ref, p_ref, s_ref):
    n_mem = mq_ref.shape[3] // XA_HEADS
    tm = x_ref.shape[1]
    sub = min(SUB_TILE, tm)
    for t in range(tm // sub):
        r0 = t * sub
        slot = t % 2
        x = x_ref[0, r0:r0 + sub, :]
        hn = _rms(x, g_ref[...]).astype(BF16)
        s_ref[slot] = _dot(hn, mq_ref[0, 0])
        for hd in range(XA_HEADS):
            s = s_ref[slot, :, hd * n_mem:(hd + 1) * n_mem]
            p = jnp.exp2(s - jnp.max(s, axis=-1, keepdims=True))
            r = 1.0 / jnp.sum(p, axis=-1, keepdims=True)
            p_ref[slot, :, hd * n_mem:(hd + 1) * n_mem] = (p * r).astype(BF16)
        o_ref[0, r0:r0 + sub, :] = x + _dot(p_ref[slot], no_ref[0, 0])


def _xattn(h, g, mq, no, layer, tm):
    b, s, d = h.shape
    hm = mq.shape[3]
    return pl.pallas_call(
        _xattn_kernel,
        grid=(b, s // tm),
        in_specs=[
            pl.BlockSpec((1, tm, d), lambda bi, i: (bi, i, 0)),
            _const_spec((1, d)),
            pl.BlockSpec((1, 1, d, hm), lambda bi, i: (layer, bi, 0, 0)),
            pl.BlockSpec((1, 1, hm, d), lambda bi, i: (layer, bi, 0, 0)),
        ],
        out_specs=pl.BlockSpec((1, tm, d), lambda bi, i: (bi, i, 0)),
        out_shape=jax.ShapeDtypeStruct(h.shape, F32),
        scratch_shapes=[pltpu.VMEM((2, min(SUB_TILE, tm), hm), BF16), pltpu.VMEM((2, min(SUB_TILE, tm), hm), F32)],
        compiler_params=_params("arbitrary", "arbitrary"),
        name="mem_xattn",
    )(h, g.reshape(1, d), mq, no)


def _gelu_exact(x):
    return 0.5 * x * (1.0 + lax.erf(x * (1.0 / math.sqrt(2.0))))


def _ffn_kernel(x_ref, xp_ref, xn_ref, g_ref, wup_ref, cw_ref, cb_ref, wdn_ref, gf_ref, o_ref,
                hn_ref, a_ref, *, nc, final_norm):
    tm = x_ref.shape[1]
    dff = wdn_ref.shape[0]
    sub = min(SUB_TILE, tm)
    _fill_normed_ext(hn_ref, x_ref[0], xp_ref[0], xn_ref[0], g_ref[...])
    nchunks = dff // nc
    for t in range(tm // sub):
        r0 = t * sub
        hn_ext = hn_ref[r0:r0 + sub + 2 * HALO, :]
        hn = hn_ref[HALO + r0:HALO + r0 + sub, :]

        def project(c):
            lo, hi = c * nc, (c + 1) * nc
            return _dot(hn, wup_ref[:, lo:hi]), _dot(hn_ext, wup_ref[:, dff + lo:dff + hi])

        cur = project(0)
        for c in range(nchunks):
            nxt = project(c + 1) if c + 1 < nchunks else None
            u, gp = cur
            lo, hi = c * nc, (c + 1) * nc
            gate = _conv3_rows(gp, cw_ref[:, lo:hi], sub) + cb_ref[:, lo:hi]
            a_ref[r0:r0 + sub, lo:hi] = (_gelu_exact(gate) * u).astype(BF16)
            cur = nxt
        acc = x_ref[0, r0:r0 + sub, :] + _dot(a_ref[r0:r0 + sub, :], wdn_ref[...])
        if final_norm:
            acc = _rms(acc, gf_ref[...])
        o_ref[0, r0:r0 + sub, :] = acc


def _conv_ffn(h, g, w_up, conv_w, conv_b, w_down, final_g, tm, nc=256):
    b, s, d = h.shape
    dff = w_down.shape[0]
    final_norm = final_g is not None
    gf = (final_g if final_norm else g).reshape(1, d)
    return pl.pallas_call(
        functools.partial(_ffn_kernel, nc=nc, final_norm=final_norm),
        grid=(b, s // tm),
        in_specs=_halo_specs(tm, s, d) + [
            _const_spec((1, d)),
            _const_spec(w_up.shape),
            _const_spec(conv_w.shape),
            _const_spec((1, dff)),
            _const_spec(w_down.shape),
            _const_spec((1, d)),
        ],
        out_specs=pl.BlockSpec((1, tm, d), lambda bi, i: (bi, i, 0)),
        out_shape=jax.ShapeDtypeStruct(h.shape, F32),
        scratch_shapes=[pltpu.VMEM((tm + 2 * HALO, d), BF16), pltpu.VMEM((tm, dff), BF16)],
        compiler_params=_params("arbitrary", "arbitrary"),
        name="conv_ffn",
    )(h, h, h, g.reshape(1, d), w_up, conv_w, conv_b.reshape(1, dff), w_down, gf)


def kernel(x, mem, mem_norm_g, mix_norm_g, w_in_ab, rpb, w_out_ab, w_in_c, conv_c, w_out_c,
           xa_norm_g, xa_wq, xa_wkv, xa_wo, ffn_norm_g, ffn_w_up, ffn_conv_w, ffn_conv_b,
           ffn_w_down, final_norm_g):
    b, s, d = x.shape
    depth = mix_norm_g.shape[0]
    n_mem = mem.shape[1]
    tm = min(SUB_TILE, s)
    tt = min(TOKEN_TILE, s)
    bf = lambda w: w.astype(BF16)

    cdft, m1, m3 = _dft_constants(s)
    mq, no = _mem_fold(mem.reshape(b * n_mem, d), mem_norm_g, bf(xa_wkv), bf(xa_wq), bf(xa_wo), b)

    h = x
    for layer in range(depth):
        j = layer // 2
        if layer % 2 == 0:
            g = mix_norm_g[layer]
            zt = _fourier_in(h, g, bf(w_in_ab[j, :, :FW]), cdft)
            qkv = _even_qkv(h.reshape(b * s, d), g, bf(w_in_ab[j, :, FW:]), tt).reshape(b, s, -1)
            yf = _fft_mix(zt, m1, m3)
            h = _na_mix(qkv, rpb[j], yf, h, bf(w_out_ab[j]))
        else:
            h = _odd_mixer(h, mix_norm_g[layer], bf(w_in_c[j]), conv_c[j], bf(w_out_c[j]), tt)
        h = _xattn(h, xa_norm_g[layer], mq, no, layer, min(2 * TOKEN_TILE, s))
        h = _conv_ffn(h, ffn_norm_g[layer], bf(ffn_w_up[layer]), ffn_conv_w[layer], ffn_conv_b[layer],
                      bf(ffn_w_down[layer]), final_norm_g if layer == depth - 1 else None, tt)
    return h
```

```python
import functools
import math

import numpy as np
import jax
import jax.numpy as jnp
from jax import lax
from jax.experimental import pallas as pl
from jax.experimental.pallas import tpu as pltpu

EPS = 1e-6
GRID_W = 64
NA_KH, NA_KW = 8, 16
NA_HEADS = 4
NA_HEAD_DIM = 128
FOURIER_GROUPS = 4
FOURIER_GROUP_DIM = 128
XA_HEADS = 4
NEG_MASK = -1e30

LANES = 128
SUBLANES = 8
HALO = 16
SUB_TILE = 512
TOKEN_TILE = 1024
VMEM_LIMIT = 56 * 1024 * 1024

F32 = jnp.float32
BF16 = jnp.bfloat16

FW = FOURIER_GROUPS * FOURIER_GROUP_DIM
NW = NA_HEADS * NA_HEAD_DIM
NI = GRID_W


def _rms(x, g):
    ms = jnp.mean(x * x, axis=-1, keepdims=True)
    return (x * lax.rsqrt(ms + EPS)) * g


def _dot(a, b):
    return jnp.dot(a, b, preferred_element_type=F32)


def _dot_nt(a, b):
    return lax.dot_general(a, b, (((1,), (1,)), ((), ())), preferred_element_type=F32)


LOG2E = 1.4426950408889634


def _softmax_pv(s, v):
    m = jnp.max(s, axis=-1, keepdims=True)
    p = jnp.exp2(s - m)
    l = jnp.sum(p, axis=-1, keepdims=True)
    return _dot(p.astype(BF16), v) / l


def _const_spec(shape):
    nd = len(shape)
    return pl.BlockSpec(shape, lambda *_: (0,) * nd, pipeline_mode=pl.Buffered(1))


def _params(*sem):
    return pltpu.CompilerParams(dimension_semantics=sem, vmem_limit_bytes=VMEM_LIMIT)


def _strided_pitch(n):
    p = -(-n // SUBLANES) * SUBLANES
    return p if (p // SUBLANES) % 2 == 1 else p + SUBLANES


def _mem_fold_kernel(m_ref, g_ref, wkv_ref, wq_ref, wo_ref, mq_ref, no_ref):
    n_mem, d = m_ref.shape
    dh = d // XA_HEADS
    mn = _rms(m_ref[...], g_ref[...]).astype(BF16)
    kv = _dot(mn, wkv_ref[0]).astype(BF16)
    for hd in range(XA_HEADS):
        k = kv[:, hd * dh:(hd + 1) * dh]
        v = kv[:, d + hd * dh:d + (hd + 1) * dh]
        mq = _dot_nt(wq_ref[0, :, hd * dh:(hd + 1) * dh], k) * (LOG2E * dh ** -0.5)
        mq_ref[0, 0, :, hd * n_mem:(hd + 1) * n_mem] = mq.astype(BF16)
        no_ref[0, 0, hd * n_mem:(hd + 1) * n_mem, :] = _dot(v, wo_ref[0, hd * dh:(hd + 1) * dh, :]).astype(BF16)


def _mem_fold(mem2d, g, wkv, wq, wo, b):
    bm, d = mem2d.shape
    n_mem = bm // b
    nl = wkv.shape[0]
    hm = XA_HEADS * n_mem
    return pl.pallas_call(
        _mem_fold_kernel,
        grid=(nl, b),
        in_specs=[
            pl.BlockSpec((n_mem, d), lambda l, bi: (bi, 0)),
            pl.BlockSpec((1, d), lambda l, bi: (0, 0)),
            pl.BlockSpec((1, d, 2 * d), lambda l, bi: (l, 0, 0)),
            pl.BlockSpec((1, d, d), lambda l, bi: (l, 0, 0)),
            pl.BlockSpec((1, d, d), lambda l, bi: (l, 0, 0)),
        ],
        out_specs=[pl.BlockSpec((1, 1, d, hm), lambda l, bi: (l, bi, 0, 0)),
                   pl.BlockSpec((1, 1, hm, d), lambda l, bi: (l, bi, 0, 0))],
        out_shape=[jax.ShapeDtypeStruct((nl, b, d, hm), BF16), jax.ShapeDtypeStruct((nl, b, hm, d), BF16)],
        compiler_params=_params("arbitrary", "arbitrary"),
        name="mem_fold",
    )(mem2d, g.reshape(1, d), wkv, wq, wo)


EIN_I = 16
EIN_CHAINS = 2


def _even_in_kernel(x_ref, g_ref, w_ref, cdft_ref, qkv_ref, zt_ref, sc_ref):
    no, d = x_ref.shape[1], x_ref.shape[3]
    gd = FOURIER_GROUP_DIM
    nslab = 2 * FOURIER_GROUPS
    ngrp8 = EIN_I // SUBLANES
    oc = no // EIN_CHAINS
    cd = cdft_ref[...]
    for ch in range(EIN_CHAINS):
        o0 = ch * oc
        x = x_ref[0, o0:o0 + oc].reshape(oc * EIN_I, d)
        hn = _rms(x, g_ref[...]).astype(BF16)
        z = _dot(hn, w_ref[...])
        q = (z[:, FW:FW + NW] * (LOG2E * NA_HEAD_DIM ** -0.5)).astype(BF16)
        qkv = jnp.concatenate([q, z[:, FW + NW:].astype(BF16)], axis=1)
        qkv_ref[0, o0:o0 + oc] = qkv.reshape(oc, EIN_I, 3 * NW)
        for grp in range(FOURIER_GROUPS):
            pq = _dot(z[:, grp * gd:(grp + 1) * gd].astype(BF16), cd)
            pq = pq.reshape(oc, EIN_I, 2 * gd)
            for g8 in range(ngrp8):
                part = pq[:, g8 * SUBLANES:(g8 + 1) * SUBLANES, :].reshape(oc * SUBLANES, 2 * gd)
                rows = pl.ds(o0 * SUBLANES, oc * SUBLANES)
                sc_ref[g8, grp, rows, :] = part[:, :gd]
                sc_ref[g8, FOURIER_GROUPS + grp, rows, :] = part[:, gd:]
    for g8 in range(ngrp8):
        for il in range(SUBLANES):
            rows = [sc_ref[g8, sl, pl.ds(il, no, stride=SUBLANES), :] for sl in range(nslab)]
            zt_ref[0, g8 * SUBLANES + il] = jnp.concatenate(rows, axis=1).astype(BF16)


def _even_in(h, g, w_in, cdft):
    b, s, d = h.shape
    no = s // NI
    h4 = h.reshape(b, no, NI, d)
    qkv, zt = pl.pallas_call(
        _even_in_kernel,
        grid=(b, NI // EIN_I),
        in_specs=[
            pl.BlockSpec((1, no, EIN_I, d), lambda bi, i: (bi, 0, i, 0)),
            _const_spec((1, d)),
            _const_spec(w_in.shape),
            _const_spec(cdft.shape),
        ],
        out_specs=[pl.BlockSpec((1, no, EIN_I, 3 * NW), lambda bi, i: (bi, 0, i, 0)),
                   pl.BlockSpec((1, EIN_I, no, 2 * FW), lambda bi, i: (bi, i, 0, 0))],
        out_shape=[jax.ShapeDtypeStruct((b, no, NI, 3 * NW), BF16),
                   jax.ShapeDtypeStruct((b, NI, no, 2 * FW), BF16)],
        scratch_shapes=[pltpu.VMEM((EIN_I // SUBLANES, 2 * FW // LANES, no * SUBLANES, LANES), F32)],
        compiler_params=_params("arbitrary", "arbitrary"),
        name="even_in_proj",
    )(h4, g.reshape(1, d), w_in, cdft)
    return qkv.reshape(b, s, 3 * NW), zt


FFT_OUT_STEPS = 2


def _fft_kernel(zt_ref, m1_ref, m3_ref, o_ref, bt_ref, yf_ref, *, pitch_b, pitch_y):
    no = zt_ref.shape[2]
    nslab = FW // LANES
    j = pl.program_id(1)

    @pl.when(j == 0)
    def _():
        def stage1(i, carry):
            zi = zt_ref[0, i]
            rhs = jnp.concatenate([zi[:, :FW], zi[:, FW:]], axis=0)
            a = _dot(m1_ref[i], rhs)
            row0 = pl.multiple_of(i * pitch_b, SUBLANES)
            for sl in range(nslab):
                bt_ref[sl, pl.ds(row0, 2 * no), :] = a[:, sl * LANES:(sl + 1) * LANES]
            return carry

        lax.fori_loop(0, NI, stage1, 0, unroll=32)

        def stage2(k1, carry):
            br = [bt_ref[sl, pl.ds(k1, NI, stride=pitch_b), :] for sl in range(nslab)]
            bi = [bt_ref[sl, pl.ds(no + k1, NI, stride=pitch_b), :] for sl in range(nslab)]
            rhs = jnp.concatenate([jnp.concatenate(br, axis=1).astype(BF16),
                                   jnp.concatenate(bi, axis=1).astype(BF16)], axis=0)
            y = _dot(m3_ref[...], rhs)
            for sl in range(nslab):
                yf_ref[sl, pl.ds(k1, NI, stride=pitch_y), :] = y[:, sl * LANES:(sl + 1) * LANES]
            return carry

        lax.fori_loop(0, no, stage2, 0, unroll=16)

    k2_step = NI // FFT_OUT_STEPS
    for k2l in range(k2_step):
        row0 = pl.multiple_of((j * k2_step + k2l) * pitch_y, SUBLANES)
        rows = jnp.concatenate([yf_ref[sl, pl.ds(row0, no), :] for sl in range(nslab)], axis=1)
        o_ref[0, k2l * no:(k2l + 1) * no, :] = rows.astype(BF16)


def _fft_mix(zt, m1, m3):
    b, _, no, _ = zt.shape
    s = NI * no
    tm = s // FFT_OUT_STEPS
    pitch_b = _strided_pitch(2 * no)
    pitch_y = _strided_pitch(no)
    return pl.pallas_call(
        functools.partial(_fft_kernel, pitch_b=pitch_b, pitch_y=pitch_y),
        grid=(b, FFT_OUT_STEPS),
        in_specs=[
            pl.BlockSpec((1, NI, no, 2 * FW), lambda bi, j: (jnp.minimum(bi + (j > 0), b - 1), 0, 0, 0)),
            _const_spec(m1.shape),
            _const_spec(m3.shape),
        ],
        out_specs=pl.BlockSpec((1, tm, FW), lambda bi, j: (bi, j, 0)),
        out_shape=jax.ShapeDtypeStruct((b, s, FW), BF16),
        scratch_shapes=[
            pltpu.VMEM((FW // LANES, NI * pitch_b, LANES), F32),
            pltpu.VMEM((FW // LANES, NI * pitch_y, LANES), F32),
        ],
        compiler_params=_params("arbitrary", "arbitrary"),
        name="fft_mix",
    )(zt, m1, m3)


def _dft_constants(s):
    n = FOURIER_GROUP_DIM
    jn = jnp.arange(n, dtype=jnp.int32)
    ang = ((jn[:, None] * jn[None, :]) % n).astype(F32) * (2.0 * math.pi / n)
    cdft = (jnp.concatenate([jnp.cos(ang), jnp.sin(ang)], axis=1) * (n ** -0.5)).astype(BF16)
    no = s // NI
    i = jnp.arange(NI, dtype=jnp.int32)[:, None, None]
    k1 = jnp.arange(no, dtype=jnp.int32)[None, :, None]
    o = jnp.arange(no, dtype=jnp.int32)[None, None, :]
    ang1 = ((k1 * (NI * o + i)) % s).astype(F32) * (2.0 * math.pi / s)
    c1, s1 = jnp.cos(ang1), jnp.sin(ang1)
    m1 = jnp.concatenate([jnp.concatenate([c1, -s1], axis=2),
                          jnp.concatenate([-s1, -c1], axis=2)], axis=1).astype(BF16)
    ji = jnp.arange(NI, dtype=jnp.int32)
    ang3 = ((ji[:, None] * ji[None, :]) % NI).astype(F32) * (2.0 * math.pi / NI)
    m3 = (jnp.concatenate([jnp.cos(ang3), jnp.sin(ang3)], axis=1) * (s ** -0.5)).astype(BF16)
    return cdft, m1, m3


NA_QROWS = 16


def _na_kernel(q_ref, k_ref, v_ref, bias_ref, yf_ref, x_ref, w_ref, o_ref, ya_ref, s_ref, *, rows):
    blk = pl.program_id(1)
    nk = NA_KH * GRID_W
    dh = NA_HEAD_DIM
    kstarts, deltas = [], []
    for a in range(NA_QROWS):
        i = blk * NA_QROWS + a
        rs = jnp.clip(i - NA_KH // 2, 0, rows - NA_KH)
        kstarts.append(pl.multiple_of(rs * GRID_W, GRID_W))
        deltas.append(i - rs)
    for a in range(NA_QROWS):
        for hd in range(NA_HEADS):
            q = q_ref[0, a * GRID_W:(a + 1) * GRID_W, hd * dh:(hd + 1) * dh]
            k = k_ref[0, pl.ds(kstarts[a], nk), hd * dh:(hd + 1) * dh]
            s_ref[a * NA_HEADS + hd] = _dot_nt(q, k) + bias_ref[deltas[a], hd]
    for a in range(NA_QROWS):
        for hd in range(NA_HEADS):
            v = v_ref[0, pl.ds(kstarts[a], nk), hd * dh:(hd + 1) * dh]
            ya_ref[a * GRID_W:(a + 1) * GRID_W, hd * dh:(hd + 1) * dh] = _softmax_pv(
                s_ref[a * NA_HEADS + hd], v).astype(BF16)
    o_ref[0] = x_ref[0] + _dot(yf_ref[0], w_ref[:FW, :]) + _dot(ya_ref[...], w_ref[FW:, :])


def _na_bias_table(rpb):
    cols = np.arange(GRID_W)
    cs = np.clip(cols - NA_KW // 2, 0, GRID_W - NA_KW)
    col_valid = (cols[None, :] >= cs[:, None]) & (cols[None, :] < cs[:, None] + NA_KW)
    col_off = np.clip(cols[None, :] - cols[:, None] + NA_KW - 1, 0, 2 * NA_KW - 2)
    delta = np.arange(NA_KH)
    row_off = np.arange(NA_KH)[None, :] - delta[:, None] + NA_KH - 1
    sel_r = (row_off[:, :, None] == np.arange(2 * NA_KH - 1)).astype(np.float32)
    sel_c = (col_off[:, :, None] == np.arange(2 * NA_KW - 1)).astype(np.float32)
    bias = jnp.einsum("hok,dro,jck->dhjrc", rpb, sel_r, sel_c, precision=lax.Precision.HIGHEST)
    bias = jnp.where(col_valid[None, None, :, None, :], bias * LOG2E, NEG_MASK)
    return bias.reshape(NA_KH, rpb.shape[0], GRID_W, NA_KH * GRID_W).astype(F32)


def _na_mix(qkv, rpb, yf, h, w_out):
    b, s, d = h.shape
    rows = s // GRID_W
    assert rows >= NA_KH and rows % NA_QROWS == 0
    table = _na_bias_table(rpb)
    tq = NA_QROWS * GRID_W
    nk = NA_KH * GRID_W
    return pl.pallas_call(
        functools.partial(_na_kernel, rows=rows),
        grid=(b, rows // NA_QROWS),
        in_specs=[
            pl.BlockSpec((1, tq, NW), lambda bi, i: (bi, i, 0)),
            pl.BlockSpec((1, s, NW), lambda bi, i: (bi, 0, 1)),
            pl.BlockSpec((1, s, NW), lambda bi, i: (bi, 0, 2)),
            _const_spec(table.shape),
            pl.BlockSpec((1, tq, FW), lambda bi, i: (bi, i, 0)),
            pl.BlockSpec((1, tq, d), lambda bi, i: (bi, i, 0)),
            _const_spec(w_out.shape),
        ],
        out_specs=pl.BlockSpec((1, tq, d), lambda bi, i: (bi, i, 0)),
        out_shape=jax.ShapeDtypeStruct(h.shape, F32),
        scratch_shapes=[pltpu.VMEM((tq, NW), BF16), pltpu.VMEM((NA_QROWS * NA_HEADS, GRID_W, nk), F32)],
        compiler_params=_params("arbitrary", "arbitrary"),
        name="neighborhood_attn",
    )(qkv, qkv, qkv, table, yf, h, w_out)


def _halo_specs(tm, s, d):
    r = tm // HALO
    last_blk = s // HALO - 1

    def prev_map(b, i):
        return (b, jnp.maximum(i * r - 1, 0), 0)

    def next_map(b, i):
        return (b, jnp.minimum((i + 1) * r, last_blk), 0)

    return [
        pl.BlockSpec((1, tm, d), lambda b, i: (b, i, 0)),
        pl.BlockSpec((1, HALO, d), prev_map),
        pl.BlockSpec((1, HALO, d), next_map),
    ]


def _fill_normed_ext(hn_ref, x, xp, xn, g):
    tm = x.shape[0]
    i = pl.program_id(1)
    first = i == 0
    last = i == pl.num_programs(1) - 1
    hp = jnp.where(first, 0.0, _rms(xp, g))
    hx = jnp.where(last, 0.0, _rms(xn, g))
    hn_ref[:HALO, :] = hp.astype(BF16)
    hn_ref[HALO:HALO + tm, :] = _rms(x, g).astype(BF16)
    hn_ref[HALO + tm:, :] = hx.astype(BF16)


def _conv3_rows(z, cw, tm):
    n = z.shape[0]
    prev = pltpu.roll(z, 1, 0)[HALO:HALO + tm]
    nxt = pltpu.roll(z, n - 1, 0)[HALO:HALO + tm]
    return prev * cw[0:1, :] + z[HALO:HALO + tm] * cw[1:2, :] + nxt * cw[2:3, :]


def _odd_mixer_kernel(x_ref, xp_ref, xn_ref, g_ref, win_ref, cw_ref, wout_ref, o_ref, hn_ref, y_ref, *, nc):
    tm, d = x_ref.shape[1], x_ref.shape[2]
    sub = min(SUB_TILE, tm)
    _fill_normed_ext(hn_ref, x_ref[0], xp_ref[0], xn_ref[0], g_ref[...])
    nchunks = d // nc
    for t in range(tm // sub):
        r0 = t * sub
        hn_ext = hn_ref[r0:r0 + sub + 2 * HALO, :]
        hn = hn_ref[HALO + r0:HALO + r0 + sub, :]

        def project(c):
            lo, hi = c * nc, (c + 1) * nc
            return (_dot(hn, win_ref[:, lo:hi]),
                    _dot(hn_ext, win_ref[:, d + lo:d + hi]),
                    _dot(hn_ext, win_ref[:, 2 * d + lo:2 * d + hi]))

        cur = project(0)
        for c in range(nchunks):
            nxt = project(c + 1) if c + 1 < nchunks else None
            gate_b, gate_c, u = cur
            lo, hi = c * nc, (c + 1) * nc
            y_ref[r0:r0 + sub, lo:hi] = (gate_b * _conv3_rows(gate_c * u, cw_ref[:, lo:hi], sub)).astype(BF16)
            cur = nxt
        o_ref[0, r0:r0 + sub, :] = x_ref[0, r0:r0 + sub, :] + _dot(y_ref[r0:r0 + sub, :], wout_ref[...])


def _odd_mixer(h, g, w_in, conv_w, w_out, tm, nc=256):
    b, s, d = h.shape
    return pl.pallas_call(
        functools.partial(_odd_mixer_kernel, nc=nc),
        grid=(b, s // tm),
        in_specs=_halo_specs(tm, s, d) + [
            _const_spec((1, d)),
            _const_spec(w_in.shape),
            _const_spec(conv_w.shape),
            _const_spec(w_out.shape),
        ],
        out_specs=pl.BlockSpec((1, tm, d), lambda bi, i: (bi, i, 0)),
        out_shape=jax.ShapeDtypeStruct(h.shape, F32),
        scratch_shapes=[pltpu.VMEM((tm + 2 * HALO, d), BF16), pltpu.VMEM((tm, d), BF16)],
        compiler_params=_params("arbitrary", "arbitrary"),
        name="odd_mixer",
    )(h, h, h, g.reshape(1, d), w_in, conv_w, w_out)


def _xattn_kernel(x_ref, g_ref, mq_ref, no_ref, o_ref, p_ref, s_ref):
    n_mem = mq_ref.shape[3] // XA_HEADS
    tm = x_ref.shape[1]
    sub = min(SUB_TILE, tm)
    for t in range(tm // sub):
        r0 = t * sub
        slot = t % 2
        x = x_ref[0, r0:r0 + sub, :]
        hn = _rms(x, g_ref[...]).astype(BF16)
        s_ref[slot] = _dot(hn, mq_ref[0, 0])
        for hd in range(XA_HEADS):
            s = s_ref[slot, :, hd * n_mem:(hd + 1) * n_mem]
            p = jnp.exp2(s - jnp.max(s, axis=-1, keepdims=True))
            r = 1.0 / jnp.sum(p, axis=-1, keepdims=True)
            p_ref[slot, :, hd * n_mem:(hd + 1) * n_mem] = (p * r).astype(BF16)
        o_ref[0, r0:r0 + sub, :] = x + _dot(p_ref[slot], no_ref[0, 0])


def _xattn(h, g, mq, no, layer, tm):
    b, s, d = h.shape
    hm = mq.shape[3]
    return pl.pallas_call(
        _xattn_kernel,
        grid=(b, s // tm),
        in_specs=[
            pl.BlockSpec((1, tm, d), lambda bi, i: (bi, i, 0)),
            _const_spec((1, d)),
            pl.BlockSpec((1, 1, d, hm), lambda bi, i: (layer, bi, 0, 0)),
            pl.BlockSpec((1, 1, hm, d), lambda bi, i: (layer, bi, 0, 0)),
        ],
        out_specs=pl.BlockSpec((1, tm, d), lambda bi, i: (bi, i, 0)),
        out_shape=jax.ShapeDtypeStruct(h.shape, F32),
        scratch_shapes=[pltpu.VMEM((2, min(SUB_TILE, tm), hm), BF16), pltpu.VMEM((2, min(SUB_TILE, tm), hm), F32)],
        compiler_params=_params("arbitrary", "arbitrary"),
        name="mem_xattn",
    )(h, g.reshape(1, d), mq, no)


def _gelu_exact(x):
    return 0.5 * x * (1.0 + lax.erf(x * (1.0 / math.sqrt(2.0))))


def _ffn_kernel(x_ref, xp_ref, xn_ref, g_ref, wup_ref, cw_ref, cb_ref, wdn_ref, gf_ref, o_ref,
                hn_ref, a_ref, *, nc, final_norm):
    tm = x_ref.shape[1]
    dff = wdn_ref.shape[0]
    sub = min(SUB_TILE, tm)
    _fill_normed_ext(hn_ref, x_ref[0], xp_ref[0], xn_ref[0], g_ref[...])
    nchunks = dff // nc
    for t in range(tm // sub):
        r0 = t * sub
        hn_ext = hn_ref[r0:r0 + sub + 2 * HALO, :]
        hn = hn_ref[HALO + r0:HALO + r0 + sub, :]

        def project(c):
            lo, hi = c * nc, (c + 1) * nc
            return _dot(hn, wup_ref[:, lo:hi]), _dot(hn_ext, wup_ref[:, dff + lo:dff + hi])

        cur = project(0)
        for c in range(nchunks):
            nxt = project(c + 1) if c + 1 < nchunks else None
            u, gp = cur
            lo, hi = c * nc, (c + 1) * nc
            gate = _conv3_rows(gp, cw_ref[:, lo:hi], sub) + cb_ref[:, lo:hi]
            a_ref[r0:r0 + sub, lo:hi] = (_gelu_exact(gate) * u).astype(BF16)
            cur = nxt
        acc = x_ref[0, r0:r0 + sub, :] + _dot(a_ref[r0:r0 + sub, :], wdn_ref[...])
        if final_norm:
            acc = _rms(acc, gf_ref[...])
        o_ref[0, r0:r0 + sub, :] = acc


def _conv_ffn(h, g, w_up, conv_w, conv_b, w_down, final_g, tm, nc=256):
    b, s, d = h.shape
    dff = w_down.shape[0]
    final_norm = final_g is not None
    gf = (final_g if final_norm else g).reshape(1, d)
    return pl.pallas_call(
        functools.partial(_ffn_kernel, nc=nc, final_norm=final_norm),
        grid=(b, s // tm),
        in_specs=_halo_specs(tm, s, d) + [
            _const_spec((1, d)),
            _const_spec(w_up.shape),
            _const_spec(conv_w.shape),
            _const_spec((1, dff)),
            _const_spec(w_down.shape),
            _const_spec((1, d)),
        ],
        out_specs=pl.BlockSpec((1, tm, d), lambda bi, i: (bi, i, 0)),
        out_shape=jax.ShapeDtypeStruct(h.shape, F32),
        scratch_shapes=[pltpu.VMEM((tm + 2 * HALO, d), BF16), pltpu.VMEM((tm, dff), BF16)],
        compiler_params=_params("arbitrary", "arbitrary"),
        name="conv_ffn",
    )(h, h, h, g.reshape(1, d), w_up, conv_w, conv_b.reshape(1, dff), w_down, gf)


def kernel(x, mem, mem_norm_g, mix_norm_g, w_in_ab, rpb, w_out_ab, w_in_c, conv_c, w_out_c,
           xa_norm_g, xa_wq, xa_wkv, xa_wo, ffn_norm_g, ffn_w_up, ffn_conv_w, ffn_conv_b,
           ffn_w_down, final_norm_g):
    b, s, d = x.shape
    depth = mix_norm_g.shape[0]
    n_mem = mem.shape[1]
    tm = min(SUB_TILE, s)
    tt = min(TOKEN_TILE, s)
    bf = lambda w: w.astype(BF16)

    cdft, m1, m3 = _dft_constants(s)
    mq, no = _mem_fold(mem.reshape(b * n_mem, d), mem_norm_g, bf(xa_wkv), bf(xa_wq), bf(xa_wo), b)

    h = x
    for layer in range(depth):
        j = layer // 2
        if layer % 2 == 0:
            g = mix_norm_g[layer]
            qkv, zt = _even_in(h, g, bf(w_in_ab[j]), cdft)
            yf = _fft_mix(zt, m1, m3)
            h = _na_mix(qkv, rpb[j], yf, h, bf(w_out_ab[j]))
        else:
            h = _odd_mixer(h, mix_norm_g[layer], bf(w_in_c[j]), conv_c[j], bf(w_out_c[j]), tt)
        h = _xattn(h, xa_norm_g[layer], mq, no, layer, min(2 * TOKEN_TILE, s))
        h = _conv_ffn(h, ffn_norm_g[layer], bf(ffn_w_up[layer]), ffn_conv_w[layer], ffn_conv_b[layer],
                      bf(ffn_w_down[layer]), final_norm_g if layer == depth - 1 else None, tt)
    return h
```

```python
import functools
import math

import numpy as np
import jax
import jax.numpy as jnp
from jax import lax
from jax.experimental import pallas as pl
from jax.experimental.pallas import tpu as pltpu

EPS = 1e-6
GRID_W = 64
NA_KH, NA_KW = 8, 16
NA_HEADS = 4
NA_HEAD_DIM = 128
FOURIER_GROUPS = 4
FOURIER_GROUP_DIM = 128
XA_HEADS = 4
NEG_MASK = -1e30

LANES = 128
SUBLANES = 8
HALO = 16
SUB_TILE = 512
TOKEN_TILE = 1024
VMEM_LIMIT = 56 * 1024 * 1024

F32 = jnp.float32
BF16 = jnp.bfloat16

FW = FOURIER_GROUPS * FOURIER_GROUP_DIM
NW = NA_HEADS * NA_HEAD_DIM
NI = GRID_W


def _rms(x, g):
    ms = jnp.mean(x * x, axis=-1, keepdims=True)
    return (x * lax.rsqrt(ms + EPS)) * g


def _dot(a, b):
    return jnp.dot(a, b, preferred_element_type=F32)


def _dot_nt(a, b):
    return lax.dot_general(a, b, (((1,), (1,)), ((), ())), preferred_element_type=F32)


LOG2E = 1.4426950408889634


def _softmax_pv(s, v):
    m = jnp.max(s, axis=-1, keepdims=True)
    p = jnp.exp2(s - m)
    l = jnp.sum(p, axis=-1, keepdims=True)
    return _dot(p.astype(BF16), v) / l


def _const_spec(shape):
    nd = len(shape)
    return pl.BlockSpec(shape, lambda *_: (0,) * nd, pipeline_mode=pl.Buffered(1))


def _params(*sem):
    return pltpu.CompilerParams(dimension_semantics=sem, vmem_limit_bytes=VMEM_LIMIT)


def _strided_pitch(n):
    p = -(-n // SUBLANES) * SUBLANES
    return p if (p // SUBLANES) % 2 == 1 else p + SUBLANES


MEM_BATCHES = 2


def _mem_fold_kernel(m_ref, g_ref, wkv_ref, wq_ref, wo_ref, mq_ref, no_ref):
    d = m_ref.shape[1]
    n_mem = m_ref.shape[0] // MEM_BATCHES
    dh = d // XA_HEADS
    for t in range(MEM_BATCHES):
        mn = _rms(m_ref[t * n_mem:(t + 1) * n_mem, :], g_ref[...]).astype(BF16)
        kv = _dot(mn, wkv_ref[0]).astype(BF16)
        for hd in range(XA_HEADS):
            k = kv[:, hd * dh:(hd + 1) * dh]
            v = kv[:, d + hd * dh:d + (hd + 1) * dh]
            mq = _dot_nt(wq_ref[0, :, hd * dh:(hd + 1) * dh], k) * (LOG2E * dh ** -0.5)
            mq_ref[0, t, :, hd * n_mem:(hd + 1) * n_mem] = mq.astype(BF16)
            no_ref[0, t, hd * n_mem:(hd + 1) * n_mem, :] = _dot(
                v, wo_ref[0, hd * dh:(hd + 1) * dh, :]).astype(BF16)


def _mem_fold(mem2d, g, wkv, wq, wo, b):
    bm, d = mem2d.shape
    n_mem = bm // b
    nl = wkv.shape[0]
    hm = XA_HEADS * n_mem
    nb = MEM_BATCHES
    assert b % nb == 0
    return pl.pallas_call(
        _mem_fold_kernel,
        grid=(nl, b // nb),
        in_specs=[
            pl.BlockSpec((nb * n_mem, d), lambda l, bi: (bi, 0)),
            pl.BlockSpec((1, d), lambda l, bi: (0, 0)),
            pl.BlockSpec((1, d, 2 * d), lambda l, bi: (l, 0, 0)),
            pl.BlockSpec((1, d, d), lambda l, bi: (l, 0, 0)),
            pl.BlockSpec((1, d, d), lambda l, bi: (l, 0, 0)),
        ],
        out_specs=[pl.BlockSpec((1, nb, d, hm), lambda l, bi: (l, bi, 0, 0)),
                   pl.BlockSpec((1, nb, hm, d), lambda l, bi: (l, bi, 0, 0))],
        out_shape=[jax.ShapeDtypeStruct((nl, b, d, hm), BF16), jax.ShapeDtypeStruct((nl, b, hm, d), BF16)],
        compiler_params=_params("arbitrary", "arbitrary"),
        name="mem_fold",
    )(mem2d, g.reshape(1, d), wkv, wq, wo)


EIN_I = 16
EIN_CHAINS = 2


def _even_in_kernel(x_ref, g_ref, w_ref, cdft_ref, qkv_ref, zt_ref, sc_ref):
    no, d = x_ref.shape[1], x_ref.shape[3]
    gd = FOURIER_GROUP_DIM
    nslab = 2 * FOURIER_GROUPS
    ngrp8 = EIN_I // SUBLANES
    oc = no // EIN_CHAINS
    cd = cdft_ref[...]
    for ch in range(EIN_CHAINS):
        o0 = ch * oc
        x = x_ref[0, o0:o0 + oc].reshape(oc * EIN_I, d)
        hn = _rms(x, g_ref[...]).astype(BF16)
        z = _dot(hn, w_ref[...])
        q = (z[:, FW:FW + NW] * (LOG2E * NA_HEAD_DIM ** -0.5)).astype(BF16)
        qkv = jnp.concatenate([q, z[:, FW + NW:].astype(BF16)], axis=1)
        qkv_ref[0, o0:o0 + oc] = qkv.reshape(oc, EIN_I, 3 * NW)
        for grp in range(FOURIER_GROUPS):
            pq = _dot(z[:, grp * gd:(grp + 1) * gd].astype(BF16), cd)
            pq = pq.reshape(oc, EIN_I, 2 * gd)
            for g8 in range(ngrp8):
                part = pq[:, g8 * SUBLANES:(g8 + 1) * SUBLANES, :].reshape(oc * SUBLANES, 2 * gd)
                rows = pl.ds(o0 * SUBLANES, oc * SUBLANES)
                sc_ref[g8, grp, rows, :] = part[:, :gd]
                sc_ref[g8, FOURIER_GROUPS + grp, rows, :] = part[:, gd:]
    for g8 in range(ngrp8):
        for il in range(SUBLANES):
            rows = [sc_ref[g8, sl, pl.ds(il, no, stride=SUBLANES), :] for sl in range(nslab)]
            zt_ref[0, g8 * SUBLANES + il] = jnp.concatenate(rows, axis=1).astype(BF16)


def _even_in(h, g, w_in, cdft):
    b, s, d = h.shape
    no = s // NI
    h4 = h.reshape(b, no, NI, d)
    qkv, zt = pl.pallas_call(
        _even_in_kernel,
        grid=(b, NI // EIN_I),
        in_specs=[
            pl.BlockSpec((1, no, EIN_I, d), lambda bi, i: (bi, 0, i, 0)),
            _const_spec((1, d)),
            _const_spec(w_in.shape),
            _const_spec(cdft.shape),
        ],
        out_specs=[pl.BlockSpec((1, no, EIN_I, 3 * NW), lambda bi, i: (bi, 0, i, 0)),
                   pl.BlockSpec((1, EIN_I, no, 2 * FW), lambda bi, i: (bi, i, 0, 0))],
        out_shape=[jax.ShapeDtypeStruct((b, no, NI, 3 * NW), BF16),
                   jax.ShapeDtypeStruct((b, NI, no, 2 * FW), BF16)],
        scratch_shapes=[pltpu.VMEM((EIN_I // SUBLANES, 2 * FW // LANES, no * SUBLANES, LANES), F32)],
        compiler_params=_params("arbitrary", "arbitrary"),
        name="even_in_proj",
    )(h4, g.reshape(1, d), w_in, cdft)
    return qkv.reshape(b, s, 3 * NW), zt


FFT_OUT_STEPS = 2


def _fft_kernel(zt_ref, m1_ref, m3_ref, o_ref, bt_ref, yf_ref, *, pitch_b, pitch_y):
    no = zt_ref.shape[2]
    nslab = FW // LANES
    j = pl.program_id(1)

    @pl.when(j == 0)
    def _():
        for i in range(NI):
            zi = zt_ref[0, i]
            rhs = jnp.concatenate([zi[:, :FW], zi[:, FW:]], axis=0)
            a = _dot(m1_ref[i], rhs)
            for sl in range(nslab):
                bt_ref[sl, i * pitch_b:i * pitch_b + 2 * no, :] = a[:, sl * LANES:(sl + 1) * LANES]

        def stage2(k1, carry):
            br = [bt_ref[sl, pl.ds(k1, NI, stride=pitch_b), :] for sl in range(nslab)]
            bi = [bt_ref[sl, pl.ds(no + k1, NI, stride=pitch_b), :] for sl in range(nslab)]
            rhs = jnp.concatenate([jnp.concatenate(br, axis=1).astype(BF16),
                                   jnp.concatenate(bi, axis=1).astype(BF16)], axis=0)
            y = _dot(m3_ref[...], rhs)
            for sl in range(nslab):
                yf_ref[sl, pl.ds(k1, NI, stride=pitch_y), :] = y[:, sl * LANES:(sl + 1) * LANES]
            return carry

        lax.fori_loop(0, no, stage2, 0, unroll=16)

    k2_step = NI // FFT_OUT_STEPS
    for k2l in range(k2_step):
        row0 = pl.multiple_of((j * k2_step + k2l) * pitch_y, SUBLANES)
        rows = jnp.concatenate([yf_ref[sl, pl.ds(row0, no), :] for sl in range(nslab)], axis=1)
        o_ref[0, k2l * no:(k2l + 1) * no, :] = rows.astype(BF16)


def _fft_mix(zt, m1, m3):
    b, _, no, _ = zt.shape
    s = NI * no
    tm = s // FFT_OUT_STEPS
    assert (2 * no) % SUBLANES == 0
    pitch_b = 2 * no + SUBLANES // 2
    pitch_y = _strided_pitch(no)
    return pl.pallas_call(
        functools.partial(_fft_kernel, pitch_b=pitch_b, pitch_y=pitch_y),
        grid=(b, FFT_OUT_STEPS),
        in_specs=[
            pl.BlockSpec((1, NI, no, 2 * FW), lambda bi, j: (jnp.minimum(bi + (j > 0), b - 1), 0, 0, 0)),
            _const_spec(m1.shape),
            _const_spec(m3.shape),
        ],
        out_specs=pl.BlockSpec((1, tm, FW), lambda bi, j: (bi, j, 0)),
        out_shape=jax.ShapeDtypeStruct((b, s, FW), BF16),
        scratch_shapes=[
            pltpu.VMEM((FW // LANES, NI * pitch_b, LANES), F32),
            pltpu.VMEM((FW // LANES, NI * pitch_y, LANES), F32),
        ],
        compiler_params=_params("arbitrary", "arbitrary"),
        name="fft_mix",
    )(zt, m1, m3)


def _dft_constants(s):
    n = FOURIER_GROUP_DIM
    jn = jnp.arange(n, dtype=jnp.int32)
    ang = ((jn[:, None] * jn[None, :]) % n).astype(F32) * (2.0 * math.pi / n)
    cdft = (jnp.concatenate([jnp.cos(ang), jnp.sin(ang)], axis=1) * (n ** -0.5)).astype(BF16)
    no = s // NI
    i = jnp.arange(NI, dtype=jnp.int32)[:, None, None]
    k1 = jnp.arange(no, dtype=jnp.int32)[None, :, None]
    o = jnp.arange(no, dtype=jnp.int32)[None, None, :]
    ang1 = ((k1 * (NI * o + i)) % s).astype(F32) * (2.0 * math.pi / s)
    c1, s1 = jnp.cos(ang1), jnp.sin(ang1)
    m1 = jnp.concatenate([jnp.concatenate([c1, -s1], axis=2),
                          jnp.concatenate([-s1, -c1], axis=2)], axis=1).astype(BF16)
    ji = jnp.arange(NI, dtype=jnp.int32)
    ang3 = ((ji[:, None] * ji[None, :]) % NI).astype(F32) * (2.0 * math.pi / NI)
    m3 = (jnp.concatenate([jnp.cos(ang3), jnp.sin(ang3)], axis=1) * (s ** -0.5)).astype(BF16)
    return cdft, m1, m3


NA_QROWS = 16


def _na_kernel(q_ref, k_ref, v_ref, bias_ref, yf_ref, x_ref, w_ref, o_ref, ya_ref, s_ref, *, rows):
    blk = pl.program_id(1)
    nk = NA_KH * GRID_W
    dh = NA_HEAD_DIM
    kstarts, deltas = [], []
    for a in range(NA_QROWS):
        i = blk * NA_QROWS + a
        rs = jnp.clip(i - NA_KH // 2, 0, rows - NA_KH)
        kstarts.append(pl.multiple_of(rs * GRID_W, GRID_W))
        deltas.append(i - rs)
    for a in range(NA_QROWS):
        for hd in range(NA_HEADS):
            q = q_ref[0, a * GRID_W:(a + 1) * GRID_W, hd * dh:(hd + 1) * dh]
            k = k_ref[0, pl.ds(kstarts[a], nk), hd * dh:(hd + 1) * dh]
            s_ref[a * NA_HEADS + hd] = _dot_nt(q, k) + bias_ref[deltas[a], hd]
    for a in range(NA_QROWS):
        for hd in range(NA_HEADS):
            v = v_ref[0, pl.ds(kstarts[a], nk), hd * dh:(hd + 1) * dh]
            ya_ref[a * GRID_W:(a + 1) * GRID_W, hd * dh:(hd + 1) * dh] = _softmax_pv(
                s_ref[a * NA_HEADS + hd], v).astype(BF16)
    o_ref[0] = x_ref[0] + _dot(yf_ref[0], w_ref[:FW, :]) + _dot(ya_ref[...], w_ref[FW:, :])


def _na_bias_table(rpb):
    cols = np.arange(GRID_W)
    cs = np.clip(cols - NA_KW // 2, 0, GRID_W - NA_KW)
    col_valid = (cols[None, :] >= cs[:, None]) & (cols[None, :] < cs[:, None] + NA_KW)
    col_off = np.clip(cols[None, :] - cols[:, None] + NA_KW - 1, 0, 2 * NA_KW - 2)
    delta = np.arange(NA_KH)
    row_off = np.arange(NA_KH)[None, :] - delta[:, None] + NA_KH - 1
    sel_r = (row_off[:, :, None] == np.arange(2 * NA_KH - 1)).astype(np.float32)
    sel_c = (col_off[:, :, None] == np.arange(2 * NA_KW - 1)).astype(np.float32)
    bias = jnp.einsum("hok,dro,jck->dhjrc", rpb, sel_r, sel_c, precision=lax.Precision.HIGHEST)
    bias = jnp.where(col_valid[None, None, :, None, :], bias * LOG2E, NEG_MASK)
    return bias.reshape(NA_KH, rpb.shape[0], GRID_W, NA_KH * GRID_W).astype(F32)


def _na_mix(qkv, rpb, yf, h, w_out):
    b, s, d = h.shape
    rows = s // GRID_W
    assert rows >= NA_KH and rows % NA_QROWS == 0
    table = _na_bias_table(rpb)
    tq = NA_QROWS * GRID_W
    nk = NA_KH * GRID_W
    return pl.pallas_call(
        functools.partial(_na_kernel, rows=rows),
        grid=(b, rows // NA_QROWS),
        in_specs=[
            pl.BlockSpec((1, tq, NW), lambda bi, i: (bi, i, 0)),
            pl.BlockSpec((1, s, NW), lambda bi, i: (bi, 0, 1)),
            pl.BlockSpec((1, s, NW), lambda bi, i: (bi, 0, 2)),
            _const_spec(table.shape),
            pl.BlockSpec((1, tq, FW), lambda bi, i: (bi, i, 0)),
            pl.BlockSpec((1, tq, d), lambda bi, i: (bi, i, 0)),
            _const_spec(w_out.shape),
        ],
        out_specs=pl.BlockSpec((1, tq, d), lambda bi, i: (bi, i, 0)),
        out_shape=jax.ShapeDtypeStruct(h.shape, F32),
        scratch_shapes=[pltpu.VMEM((tq, NW), BF16), pltpu.VMEM((NA_QROWS * NA_HEADS, GRID_W, nk), F32)],
        compiler_params=_params("arbitrary", "arbitrary"),
        name="neighborhood_attn",
    )(qkv, qkv, qkv, table, yf, h, w_out)


def _halo_specs(tm, s, d):
    r = tm // HALO
    last_blk = s // HALO - 1

    def prev_map(b, i):
        return (b, jnp.maximum(i * r - 1, 0), 0)

    def next_map(b, i):
        return (b, jnp.minimum((i + 1) * r, last_blk), 0)

    return [
        pl.BlockSpec((1, tm, d), lambda b, i: (b, i, 0)),
        pl.BlockSpec((1, HALO, d), prev_map),
        pl.BlockSpec((1, HALO, d), next_map),
    ]


def _fill_normed_ext(hn_ref, x, xp, xn, g):
    tm = x.shape[0]
    i = pl.program_id(1)
    first = i == 0
    last = i == pl.num_programs(1) - 1
    hp = jnp.where(first, 0.0, _rms(xp, g))
    hx = jnp.where(last, 0.0, _rms(xn, g))
    hn_ref[:HALO, :] = hp.astype(BF16)
    hn_ref[HALO:HALO + tm, :] = _rms(x, g).astype(BF16)
    hn_ref[HALO + tm:, :] = hx.astype(BF16)


def _conv3_rows(z, cw, tm):
    n = z.shape[0]
    prev = pltpu.roll(z, 1, 0)[HALO:HALO + tm]
    nxt = pltpu.roll(z, n - 1, 0)[HALO:HALO + tm]
    return prev * cw[0:1, :] + z[HALO:HALO + tm] * cw[1:2, :] + nxt * cw[2:3, :]


def _odd_mixer_kernel(x_ref, xp_ref, xn_ref, g_ref, win_ref, cw_ref, wout_ref, o_ref, hn_ref, y_ref, *, nc):
    tm, d = x_ref.shape[1], x_ref.shape[2]
    sub = min(SUB_TILE, tm)
    _fill_normed_ext(hn_ref, x_ref[0], xp_ref[0], xn_ref[0], g_ref[...])
    nchunks = d // nc
    for t in range(tm // sub):
        r0 = t * sub
        hn_ext = hn_ref[r0:r0 + sub + 2 * HALO, :]
        hn = hn_ref[HALO + r0:HALO + r0 + sub, :]

        def project(c):
            lo, hi = c * nc, (c + 1) * nc
            return (_dot(hn, win_ref[:, lo:hi]),
                    _dot(hn_ext, win_ref[:, d + lo:d + hi]),
                    _dot(hn_ext, win_ref[:, 2 * d + lo:2 * d + hi]))

        cur = project(0)
        for c in range(nchunks):
            nxt = project(c + 1) if c + 1 < nchunks else None
            gate_b, gate_c, u = cur
            lo, hi = c * nc, (c + 1) * nc
            y_ref[r0:r0 + sub, lo:hi] = (gate_b * _conv3_rows(gate_c * u, cw_ref[:, lo:hi], sub)).astype(BF16)
            cur = nxt
        o_ref[0, r0:r0 + sub, :] = x_ref[0, r0:r0 + sub, :] + _dot(y_ref[r0:r0 + sub, :], wout_ref[...])


def _odd_mixer(h, g, w_in, conv_w, w_out, tm, nc=256):
    b, s, d = h.shape
    return pl.pallas_call(
        functools.partial(_odd_mixer_kernel, nc=nc),
        grid=(b, s // tm),
        in_specs=_halo_specs(tm, s, d) + [
            _const_spec((1, d)),
            _const_spec(w_in.shape),
            _const_spec(conv_w.shape),
            _const_spec(w_out.shape),
        ],
        out_specs=pl.BlockSpec((1, tm, d), lambda bi, i: (bi, i, 0)),
        out_shape=jax.ShapeDtypeStruct(h.shape, F32),
        scratch_shapes=[pltpu.VMEM((tm + 2 * HALO, d), BF16), pltpu.VMEM((tm, d), BF16)],
        compiler_params=_params("arbitrary", "arbitrary"),
        name="odd_mixer",
    )(h, h, h, g.reshape(1, d), w_in, conv_w, w_out)


def _xattn_kernel(x_ref, g_ref, mq_ref, no_ref, o_ref, p_ref, s_ref):
    n_mem = mq_ref.shape[3] // XA_HEADS
    tm = x_ref.shape[1]
    sub = min(SUB_TILE, tm)
    for t in range(tm // sub):
        r0 = t * sub
        slot = t % 2
        x = x_ref[0, r0:r0 + sub, :]
        hn = _rms(x, g_ref[...]).astype(BF16)
        s_ref[slot] = _dot(hn, mq_ref[0, 0])
        for hd in range(XA_HEADS):
            s = s_ref[slot, :, hd * n_mem:(hd + 1) * n_mem]
            p = jnp.exp2(s - jnp.max(s, axis=-1, keepdims=True))
            r = 1.0 / jnp.sum(p, axis=-1, keepdims=True)
            p_ref[slot, :, hd * n_mem:(hd + 1) * n_mem] = (p * r).astype(BF16)
        o_ref[0, r0:r0 + sub, :] = x + _dot(p_ref[slot], no_ref[0, 0])


def _xattn(h, g, mq, no, layer, tm):
    b, s, d = h.shape
    hm = mq.shape[3]
    return pl.pallas_call(
        _xattn_kernel,
        grid=(b, s // tm),
        in_specs=[
            pl.BlockSpec((1, tm, d), lambda bi, i: (bi, i, 0)),
            _const_spec((1, d)),
            pl.BlockSpec((1, 1, d, hm), lambda bi, i: (layer, bi, 0, 0)),
            pl.BlockSpec((1, 1, hm, d), lambda bi, i: (layer, bi, 0, 0)),
        ],
        out_specs=pl.BlockSpec((1, tm, d), lambda bi, i: (bi, i, 0)),
        out_shape=jax.ShapeDtypeStruct(h.shape, F32),
        scratch_shapes=[pltpu.VMEM((2, min(SUB_TILE, tm), hm), BF16), pltpu.VMEM((2, min(SUB_TILE, tm), hm), F32)],
        compiler_params=_params("arbitrary", "arbitrary"),
        name="mem_xattn",
    )(h, g.reshape(1, d), mq, no)


def _gelu_exact(x):
    return 0.5 * x * (1.0 + lax.erf(x * (1.0 / math.sqrt(2.0))))


def _ffn_kernel(x_ref, xp_ref, xn_ref, g_ref, wup_ref, cw_ref, cb_ref, wdn_ref, gf_ref, o_ref,
                hn_ref, a_ref, *, nc, final_norm):
    tm = x_ref.shape[1]
    dff = wdn_ref.shape[0]
    sub = min(SUB_TILE, tm)
    _fill_normed_ext(hn_ref, x_ref[0], xp_ref[0], xn_ref[0], g_ref[...])
    nchunks = dff // nc
    for t in range(tm // sub):
        r0 = t * sub
        hn_ext = hn_ref[r0:r0 + sub + 2 * HALO, :]
        hn = hn_ref[HALO + r0:HALO + r0 + sub, :]

        def project(c):
            lo, hi = c * nc, (c + 1) * nc
            return _dot(hn, wup_ref[:, lo:hi]), _dot(hn_ext, wup_ref[:, dff + lo:dff + hi])

        cur = project(0)
        for c in range(nchunks):
            nxt = project(c + 1) if c + 1 < nchunks else None
            u, gp = cur
            lo, hi = c * nc, (c + 1) * nc
            gate = _conv3_rows(gp, cw_ref[:, lo:hi], sub) + cb_ref[:, lo:hi]
            a_ref[r0:r0 + sub, lo:hi] = (_gelu_exact(gate) * u).astype(BF16)
            cur = nxt
        acc = x_ref[0, r0:r0 + sub, :] + _dot(a_ref[r0:r0 + sub, :], wdn_ref[...])
        if final_norm:
            acc = _rms(acc, gf_ref[...])
        o_ref[0, r0:r0 + sub, :] = acc


def _conv_ffn(h, g, w_up, conv_w, conv_b, w_down, final_g, tm, nc=256):
    b, s, d = h.shape
    dff = w_down.shape[0]
    final_norm = final_g is not None
    gf = (final_g if final_norm else g).reshape(1, d)
    return pl.pallas_call(
        functools.partial(_ffn_kernel, nc=nc, final_norm=final_norm),
        grid=(b, s // tm),
        in_specs=_halo_specs(tm, s, d) + [
            _const_spec((1, d)),
            _const_spec(w_up.shape),
            _const_spec(conv_w.shape),
            _const_spec((1, dff)),
            _const_spec(w_down.shape),
            _const_spec((1, d)),
        ],
        out_specs=pl.BlockSpec((1, tm, d), lambda bi, i: (bi, i, 0)),
        out_shape=jax.ShapeDtypeStruct(h.shape, F32),
        scratch_shapes=[pltpu.VMEM((tm + 2 * HALO, d), BF16), pltpu.VMEM((tm, dff), BF16)],
        compiler_params=_params("arbitrary", "arbitrary"),
        name="conv_ffn",
    )(h, h, h, g.reshape(1, d), w_up, conv_w, conv_b.reshape(1, dff), w_down, gf)


def kernel(x, mem, mem_norm_g, mix_norm_g, w_in_ab, rpb, w_out_ab, w_in_c, conv_c, w_out_c,
           xa_norm_g, xa_wq, xa_wkv, xa_wo, ffn_norm_g, ffn_w_up, ffn_conv_w, ffn_conv_b,
           ffn_w_down, final_norm_g):
    b, s, d = x.shape
    depth = mix_norm_g.shape[0]
    n_mem = mem.shape[1]
    tm = min(SUB_TILE, s)
    tt = min(TOKEN_TILE, s)
    bf = lambda w: w.astype(BF16)

    cdft, m1, m3 = _dft_constants(s)
    mq, no = _mem_fold(mem.reshape(b * n_mem, d), mem_norm_g, bf(xa_wkv), bf(xa_wq), bf(xa_wo), b)

    h = x
    for layer in range(depth):
        j = layer // 2
        if layer % 2 == 0:
            g = mix_norm_g[layer]
            qkv, zt = _even_in(h, g, bf(w_in_ab[j]), cdft)
            yf = _fft_mix(zt, m1, m3)
            h = _na_mix(qkv, rpb[j], yf, h, bf(w_out_ab[j]))
        else:
            h = _odd_mixer(h, mix_norm_g[layer], bf(w_in_c[j]), conv_c[j], bf(w_out_c[j]), tt)
        h = _xattn(h, xa_norm_g[layer], mq, no, layer, min(2 * TOKEN_TILE, s))
        h = _conv_ffn(h, ffn_norm_g[layer], bf(ffn_w_up[layer]), ffn_conv_w[layer], ffn_conv_b[layer],
                      bf(ffn_w_down[layer]), final_norm_g if layer == depth - 1 else None, tt)
    return h
```

```python
import functools
import math

import numpy as np
import jax
import jax.numpy as jnp
from jax import lax
from jax.experimental import pallas as pl
from jax.experimental.pallas import tpu as pltpu

EPS = 1e-6
GRID_W = 64
NA_KH, NA_KW = 8, 16
NA_HEADS = 4
NA_HEAD_DIM = 128
FOURIER_GROUPS = 4
FOURIER_GROUP_DIM = 128
XA_HEADS = 4
NEG_MASK = -1e30

LANES = 128
SUBLANES = 8
HALO = 16
SUB_TILE = 512
TOKEN_TILE = 1024
VMEM_LIMIT = 56 * 1024 * 1024

F32 = jnp.float32
BF16 = jnp.bfloat16

FW = FOURIER_GROUPS * FOURIER_GROUP_DIM
NW = NA_HEADS * NA_HEAD_DIM
NI = GRID_W


def _rms(x, g):
    ms = jnp.mean(x * x, axis=-1, keepdims=True)
    return (x * lax.rsqrt(ms + EPS)) * g


def _dot(a, b):
    return jnp.dot(a, b, preferred_element_type=F32)


def _dot_nt(a, b):
    return lax.dot_general(a, b, (((1,), (1,)), ((), ())), preferred_element_type=F32)


LOG2E = 1.4426950408889634


def _softmax_pv(s, v):
    m = jnp.max(s, axis=-1, keepdims=True)
    p = jnp.exp2(s - m)
    l = jnp.sum(p, axis=-1, keepdims=True)
    return _dot(p.astype(BF16), v) / l


def _const_spec(shape):
    nd = len(shape)
    return pl.BlockSpec(shape, lambda *_: (0,) * nd, pipeline_mode=pl.Buffered(1))


def _params(*sem):
    return pltpu.CompilerParams(dimension_semantics=sem, vmem_limit_bytes=VMEM_LIMIT)


def _strided_pitch(n):
    p = -(-n // SUBLANES) * SUBLANES
    return p if (p // SUBLANES) % 2 == 1 else p + SUBLANES


MEM_BATCHES = 2


def _mem_fold_kernel(m_ref, g_ref, wkv_ref, wq_ref, wo_ref, mq_ref, no_ref):
    d = m_ref.shape[1]
    n_mem = m_ref.shape[0] // MEM_BATCHES
    dh = d // XA_HEADS
    for t in range(MEM_BATCHES):
        mn = _rms(m_ref[t * n_mem:(t + 1) * n_mem, :], g_ref[...]).astype(BF16)
        kv = _dot(mn, wkv_ref[0]).astype(BF16)
        for hd in range(XA_HEADS):
            k = kv[:, hd * dh:(hd + 1) * dh]
            v = kv[:, d + hd * dh:d + (hd + 1) * dh]
            mq = _dot_nt(wq_ref[0, :, hd * dh:(hd + 1) * dh], k) * (LOG2E * dh ** -0.5)
            mq_ref[0, t, :, hd * n_mem:(hd + 1) * n_mem] = mq.astype(BF16)
            no_ref[0, t, hd * n_mem:(hd + 1) * n_mem, :] = _dot(
                v, wo_ref[0, hd * dh:(hd + 1) * dh, :]).astype(BF16)


def _mem_fold(mem2d, g, wkv, wq, wo, b):
    bm, d = mem2d.shape
    n_mem = bm // b
    nl = wkv.shape[0]
    hm = XA_HEADS * n_mem
    nb = MEM_BATCHES
    assert b % nb == 0
    return pl.pallas_call(
        _mem_fold_kernel,
        grid=(nl, b // nb),
        in_specs=[
            pl.BlockSpec((nb * n_mem, d), lambda l, bi: (bi, 0)),
            pl.BlockSpec((1, d), lambda l, bi: (0, 0)),
            pl.BlockSpec((1, d, 2 * d), lambda l, bi: (l, 0, 0)),
            pl.BlockSpec((1, d, d), lambda l, bi: (l, 0, 0)),
            pl.BlockSpec((1, d, d), lambda l, bi: (l, 0, 0)),
        ],
        out_specs=[pl.BlockSpec((1, nb, d, hm), lambda l, bi: (l, bi, 0, 0)),
                   pl.BlockSpec((1, nb, hm, d), lambda l, bi: (l, bi, 0, 0))],
        out_shape=[jax.ShapeDtypeStruct((nl, b, d, hm), BF16), jax.ShapeDtypeStruct((nl, b, hm, d), BF16)],
        compiler_params=_params("arbitrary", "arbitrary"),
        name="mem_fold",
    )(mem2d, g.reshape(1, d), wkv, wq, wo)


EIN_I = 16
EIN_CHAINS = 4


def _even_in_kernel(x_ref, g_ref, w_ref, cdft_ref, qkv_ref, zt_ref, sc_ref):
    no, d = x_ref.shape[1], x_ref.shape[3]
    gd = FOURIER_GROUP_DIM
    nslab = 2 * FOURIER_GROUPS
    ngrp8 = EIN_I // SUBLANES
    oc = no // EIN_CHAINS
    cd = cdft_ref[...]
    for ch in range(EIN_CHAINS):
        o0 = ch * oc
        x = x_ref[0, o0:o0 + oc].reshape(oc * EIN_I, d)
        hn = _rms(x, g_ref[...]).astype(BF16)
        z = _dot(hn, w_ref[...])
        q = (z[:, FW:FW + NW] * (LOG2E * NA_HEAD_DIM ** -0.5)).astype(BF16)
        qkv = jnp.concatenate([q, z[:, FW + NW:].astype(BF16)], axis=1)
        qkv_ref[0, o0:o0 + oc] = qkv.reshape(oc, EIN_I, 3 * NW)
        for grp in range(FOURIER_GROUPS):
            pq = _dot(z[:, grp * gd:(grp + 1) * gd].astype(BF16), cd)
            pq = pq.reshape(oc, EIN_I, 2 * gd)
            for g8 in range(ngrp8):
                part = pq[:, g8 * SUBLANES:(g8 + 1) * SUBLANES, :].reshape(oc * SUBLANES, 2 * gd)
                rows = pl.ds(o0 * SUBLANES, oc * SUBLANES)
                sc_ref[g8, grp, rows, :] = part[:, :gd]
                sc_ref[g8, FOURIER_GROUPS + grp, rows, :] = part[:, gd:]
    for g8 in range(ngrp8):
        for il in range(SUBLANES):
            rows = [sc_ref[g8, sl, pl.ds(il, no, stride=SUBLANES), :] for sl in range(nslab)]
            zt_ref[0, g8 * SUBLANES + il] = jnp.concatenate(rows, axis=1).astype(BF16)


def _even_in(h, g, w_in, cdft):
    b, s, d = h.shape
    no = s // NI
    h4 = h.reshape(b, no, NI, d)
    qkv, zt = pl.pallas_call(
        _even_in_kernel,
        grid=(b, NI // EIN_I),
        in_specs=[
            pl.BlockSpec((1, no, EIN_I, d), lambda bi, i: (bi, 0, i, 0)),
            _const_spec((1, d)),
            _const_spec(w_in.shape),
            _const_spec(cdft.shape),
        ],
        out_specs=[pl.BlockSpec((1, no, EIN_I, 3 * NW), lambda bi, i: (bi, 0, i, 0)),
                   pl.BlockSpec((1, EIN_I, no, 2 * FW), lambda bi, i: (bi, i, 0, 0))],
        out_shape=[jax.ShapeDtypeStruct((b, no, NI, 3 * NW), BF16),
                   jax.ShapeDtypeStruct((b, NI, no, 2 * FW), BF16)],
        scratch_shapes=[pltpu.VMEM((EIN_I // SUBLANES, 2 * FW // LANES, no * SUBLANES, LANES), F32)],
        compiler_params=_params("arbitrary", "arbitrary"),
        name="even_in_proj",
    )(h4, g.reshape(1, d), w_in, cdft)
    return qkv.reshape(b, s, 3 * NW), zt


FFT_OUT_STEPS = 2


def _fft_kernel(zt_ref, m1_ref, m3_ref, o_ref, bt_ref, yf_ref, *, pitch_b, pitch_y):
    no = zt_ref.shape[2]
    nslab = FW // LANES
    j = pl.program_id(1)

    @pl.when(j == 0)
    def _():
        for i in range(NI):
            zi = zt_ref[0, i]
            rhs = jnp.concatenate([zi[:, :FW], zi[:, FW:]], axis=0)
            a = _dot(m1_ref[i], rhs)
            for sl in range(nslab):
                bt_ref[sl, i * pitch_b:i * pitch_b + 2 * no, :] = a[:, sl * LANES:(sl + 1) * LANES]

        def stage2(k1, carry):
            br = [bt_ref[sl, pl.ds(k1, NI, stride=pitch_b), :] for sl in range(nslab)]
            bi = [bt_ref[sl, pl.ds(no + k1, NI, stride=pitch_b), :] for sl in range(nslab)]
            rhs = jnp.concatenate([jnp.concatenate(br, axis=1).astype(BF16),
                                   jnp.concatenate(bi, axis=1).astype(BF16)], axis=0)
            y = _dot(m3_ref[...], rhs)
            for sl in range(nslab):
                yf_ref[sl, pl.ds(k1, NI, stride=pitch_y), :] = y[:, sl * LANES:(sl + 1) * LANES]
            return carry

        lax.fori_loop(0, no, stage2, 0, unroll=16)

    k2_step = NI // FFT_OUT_STEPS
    for k2l in range(k2_step):
        row0 = pl.multiple_of((j * k2_step + k2l) * pitch_y, SUBLANES)
        rows = jnp.concatenate([yf_ref[sl, pl.ds(row0, no), :] for sl in range(nslab)], axis=1)
        o_ref[0, k2l * no:(k2l + 1) * no, :] = rows.astype(BF16)


def _fft_mix(zt, m1, m3):
    b, _, no, _ = zt.shape
    s = NI * no
    tm = s // FFT_OUT_STEPS
    assert (2 * no) % SUBLANES == 0
    pitch_b = 2 * no + SUBLANES // 2
    pitch_y = _strided_pitch(no)
    return pl.pallas_call(
        functools.partial(_fft_kernel, pitch_b=pitch_b, pitch_y=pitch_y),
        grid=(b, FFT_OUT_STEPS),
        in_specs=[
            pl.BlockSpec((1, NI, no, 2 * FW), lambda bi, j: (jnp.minimum(bi + (j > 0), b - 1), 0, 0, 0)),
            _const_spec(m1.shape),
            _const_spec(m3.shape),
        ],
        out_specs=pl.BlockSpec((1, tm, FW), lambda bi, j: (bi, j, 0)),
        out_shape=jax.ShapeDtypeStruct((b, s, FW), BF16),
        scratch_shapes=[
            pltpu.VMEM((FW // LANES, NI * pitch_b, LANES), F32),
            pltpu.VMEM((FW // LANES, NI * pitch_y, LANES), F32),
        ],
        compiler_params=_params("arbitrary", "arbitrary"),
        name="fft_mix",
    )(zt, m1, m3)


def _dft_constants(s):
    n = FOURIER_GROUP_DIM
    jn = jnp.arange(n, dtype=jnp.int32)
    ang = ((jn[:, None] * jn[None, :]) % n).astype(F32) * (2.0 * math.pi / n)
    cdft = (jnp.concatenate([jnp.cos(ang), jnp.sin(ang)], axis=1) * (n ** -0.5)).astype(BF16)
    no = s // NI
    i = jnp.arange(NI, dtype=jnp.int32)[:, None, None]
    k1 = jnp.arange(no, dtype=jnp.int32)[None, :, None]
    o = jnp.arange(no, dtype=jnp.int32)[None, None, :]
    ang1 = ((k1 * (NI * o + i)) % s).astype(F32) * (2.0 * math.pi / s)
    c1, s1 = jnp.cos(ang1), jnp.sin(ang1)
    m1 = jnp.concatenate([jnp.concatenate([c1, -s1], axis=2),
                          jnp.concatenate([-s1, -c1], axis=2)], axis=1).astype(BF16)
    ji = jnp.arange(NI, dtype=jnp.int32)
    ang3 = ((ji[:, None] * ji[None, :]) % NI).astype(F32) * (2.0 * math.pi / NI)
    m3 = (jnp.concatenate([jnp.cos(ang3), jnp.sin(ang3)], axis=1) * (s ** -0.5)).astype(BF16)
    return cdft, m1, m3


NA_QROWS = 16


def _na_kernel(q_ref, k_ref, v_ref, bias_ref, yf_ref, x_ref, w_ref, o_ref, ya_ref, s_ref, *, rows):
    blk = pl.program_id(1)
    nk = NA_KH * GRID_W
    dh = NA_HEAD_DIM
    kstarts, deltas = [], []
    for a in range(NA_QROWS):
        i = blk * NA_QROWS + a
        rs = jnp.clip(i - NA_KH // 2, 0, rows - NA_KH)
        kstarts.append(pl.multiple_of(rs * GRID_W, GRID_W))
        deltas.append(i - rs)
    for a in range(NA_QROWS):
        for hd in range(NA_HEADS):
            q = q_ref[0, a * GRID_W:(a + 1) * GRID_W, hd * dh:(hd + 1) * dh]
            k = k_ref[0, pl.ds(kstarts[a], nk), hd * dh:(hd + 1) * dh]
            s_ref[a * NA_HEADS + hd] = _dot_nt(q, k) + bias_ref[deltas[a], hd]
    for a in range(NA_QROWS):
        for hd in range(NA_HEADS):
            v = v_ref[0, pl.ds(kstarts[a], nk), hd * dh:(hd + 1) * dh]
            ya_ref[a * GRID_W:(a + 1) * GRID_W, hd * dh:(hd + 1) * dh] = _softmax_pv(
                s_ref[a * NA_HEADS + hd], v).astype(BF16)
    o_ref[0] = x_ref[0] + _dot(yf_ref[0], w_ref[:FW, :]) + _dot(ya_ref[...], w_ref[FW:, :])


def _na_bias_table(rpb):
    cols = np.arange(GRID_W)
    cs = np.clip(cols - NA_KW // 2, 0, GRID_W - NA_KW)
    col_valid = (cols[None, :] >= cs[:, None]) & (cols[None, :] < cs[:, None] + NA_KW)
    col_off = np.clip(cols[None, :] - cols[:, None] + NA_KW - 1, 0, 2 * NA_KW - 2)
    delta = np.arange(NA_KH)
    row_off = np.arange(NA_KH)[None, :] - delta[:, None] + NA_KH - 1
    sel_r = (row_off[:, :, None] == np.arange(2 * NA_KH - 1)).astype(np.float32)
    sel_c = (col_off[:, :, None] == np.arange(2 * NA_KW - 1)).astype(np.float32)
    bias = jnp.einsum("hok,dro,jck->dhjrc", rpb, sel_r, sel_c, precision=lax.Precision.HIGHEST)
    bias = jnp.where(col_valid[None, None, :, None, :], bias * LOG2E, NEG_MASK)
    return bias.reshape(NA_KH, rpb.shape[0], GRID_W, NA_KH * GRID_W).astype(F32)


def _na_mix(qkv, rpb, yf, h, w_out):
    b, s, d = h.shape
    rows = s // GRID_W
    assert rows >= NA_KH and rows % NA_QROWS == 0
    table = _na_bias_table(rpb)
    tq = NA_QROWS * GRID_W
    nk = NA_KH * GRID_W
    return pl.pallas_call(
        functools.partial(_na_kernel, rows=rows),
        grid=(b, rows // NA_QROWS),
        in_specs=[
            pl.BlockSpec((1, tq, NW), lambda bi, i: (bi, i, 0)),
            pl.BlockSpec((1, s, NW), lambda bi, i: (bi, 0, 1)),
            pl.BlockSpec((1, s, NW), lambda bi, i: (bi, 0, 2)),
            _const_spec(table.shape),
            pl.BlockSpec((1, tq, FW), lambda bi, i: (bi, i, 0)),
            pl.BlockSpec((1, tq, d), lambda bi, i: (bi, i, 0)),
            _const_spec(w_out.shape),
        ],
        out_specs=pl.BlockSpec((1, tq, d), lambda bi, i: (bi, i, 0)),
        out_shape=jax.ShapeDtypeStruct(h.shape, F32),
        scratch_shapes=[pltpu.VMEM((tq, NW), BF16), pltpu.VMEM((NA_QROWS * NA_HEADS, GRID_W, nk), F32)],
        compiler_params=_params("arbitrary", "arbitrary"),
        name="neighborhood_attn",
    )(qkv, qkv, qkv, table, yf, h, w_out)


def _halo_specs(tm, s, d):
    r = tm // HALO
    last_blk = s // HALO - 1

    def prev_map(b, i):
        return (b, jnp.maximum(i * r - 1, 0), 0)

    def next_map(b, i):
        return (b, jnp.minimum((i + 1) * r, last_blk), 0)

    return [
        pl.BlockSpec((1, tm, d), lambda b, i: (b, i, 0)),
        pl.BlockSpec((1, HALO, d), prev_map),
        pl.BlockSpec((1, HALO, d), next_map),
    ]


def _fill_normed_ext(hn_ref, x, xp, xn, g):
    tm = x.shape[0]
    i = pl.program_id(1)
    first = i == 0
    last = i == pl.num_programs(1) - 1
    hp = jnp.where(first, 0.0, _rms(xp, g))
    hx = jnp.where(last, 0.0, _rms(xn, g))
    hn_ref[:HALO, :] = hp.astype(BF16)
    hn_ref[HALO:HALO + tm, :] = _rms(x, g).astype(BF16)
    hn_ref[HALO + tm:, :] = hx.astype(BF16)


def _conv3_rows(z, cw, tm):
    n = z.shape[0]
    prev = pltpu.roll(z, 1, 0)[HALO:HALO + tm]
    nxt = pltpu.roll(z, n - 1, 0)[HALO:HALO + tm]
    return prev * cw[0:1, :] + z[HALO:HALO + tm] * cw[1:2, :] + nxt * cw[2:3, :]


def _odd_mixer_kernel(x_ref, xp_ref, xn_ref, g_ref, win_ref, cw_ref, wout_ref, o_ref, hn_ref, y_ref, *, nc):
    tm, d = x_ref.shape[1], x_ref.shape[2]
    sub = min(SUB_TILE, tm)
    _fill_normed_ext(hn_ref, x_ref[0], xp_ref[0], xn_ref[0], g_ref[...])
    nchunks = d // nc
    for t in range(tm // sub):
        r0 = t * sub
        hn_ext = hn_ref[r0:r0 + sub + 2 * HALO, :]
        hn = hn_ref[HALO + r0:HALO + r0 + sub, :]

        def project(c):
            lo, hi = c * nc, (c + 1) * nc
            return (_dot(hn, win_ref[:, lo:hi]),
                    _dot(hn_ext, win_ref[:, d + lo:d + hi]),
                    _dot(hn_ext, win_ref[:, 2 * d + lo:2 * d + hi]))

        cur = project(0)
        for c in range(nchunks):
            nxt = project(c + 1) if c + 1 < nchunks else None
            gate_b, gate_c, u = cur
            lo, hi = c * nc, (c + 1) * nc
            y_ref[r0:r0 + sub, lo:hi] = (gate_b * _conv3_rows(gate_c * u, cw_ref[:, lo:hi], sub)).astype(BF16)
            cur = nxt
        o_ref[0, r0:r0 + sub, :] = x_ref[0, r0:r0 + sub, :] + _dot(y_ref[r0:r0 + sub, :], wout_ref[...])


def _odd_mixer(h, g, w_in, conv_w, w_out, tm, nc=256):
    b, s, d = h.shape
    return pl.pallas_call(
        functools.partial(_odd_mixer_kernel, nc=nc),
        grid=(b, s // tm),
        in_specs=_halo_specs(tm, s, d) + [
            _const_spec((1, d)),
            _const_spec(w_in.shape),
            _const_spec(conv_w.shape),
            _const_spec(w_out.shape),
        ],
        out_specs=pl.BlockSpec((1, tm, d), lambda bi, i: (bi, i, 0)),
        out_shape=jax.ShapeDtypeStruct(h.shape, F32),
        scratch_shapes=[pltpu.VMEM((tm + 2 * HALO, d), BF16), pltpu.VMEM((tm, d), BF16)],
        compiler_params=_params("arbitrary", "arbitrary"),
        name="odd_mixer",
    )(h, h, h, g.reshape(1, d), w_in, conv_w, w_out)


def _xattn_kernel(x_ref, g_ref, mq_ref, no_ref, o_ref, p_ref, s_ref):
    n_mem = mq_ref.shape[3] // XA_HEADS
    tm = x_ref.shape[1]
    sub = min(SUB_TILE, tm)
    for t in range(tm // sub):
        r0 = t * sub
        slot = t % 2
        x = x_ref[0, r0:r0 + sub, :]
        hn = _rms(x, g_ref[...]).astype(BF16)
        s_ref[slot] = _dot(hn, mq_ref[0, 0])
        for hd in range(XA_HEADS):
            s = s_ref[slot, :, hd * n_mem:(hd + 1) * n_mem]
            p = jnp.exp2(s - jnp.max(s, axis=-1, keepdims=True))
            r = 1.0 / jnp.sum(p, axis=-1, keepdims=True)
            p_ref[slot, :, hd * n_mem:(hd + 1) * n_mem] = (p * r).astype(BF16)
        o_ref[0, r0:r0 + sub, :] = x + _dot(p_ref[slot], no_ref[0, 0])


def _xattn(h, g, mq, no, layer, tm):
    b, s, d = h.shape
    hm = mq.shape[3]
    return pl.pallas_call(
        _xattn_kernel,
        grid=(b, s // tm),
        in_specs=[
            pl.BlockSpec((1, tm, d), lambda bi, i: (bi, i, 0)),
            _const_spec((1, d)),
            pl.BlockSpec((1, 1, d, hm), lambda bi, i: (layer, bi, 0, 0)),
            pl.BlockSpec((1, 1, hm, d), lambda bi, i: (layer, bi, 0, 0)),
        ],
        out_specs=pl.BlockSpec((1, tm, d), lambda bi, i: (bi, i, 0)),
        out_shape=jax.ShapeDtypeStruct(h.shape, F32),
        scratch_shapes=[pltpu.VMEM((2, min(SUB_TILE, tm), hm), BF16), pltpu.VMEM((2, min(SUB_TILE, tm), hm), F32)],
        compiler_params=_params("arbitrary", "arbitrary"),
        name="mem_xattn",
    )(h, g.reshape(1, d), mq, no)


def _gelu_exact(x):
    return 0.5 * x * (1.0 + lax.erf(x * (1.0 / math.sqrt(2.0))))


def _ffn_kernel(x_ref, xp_ref, xn_ref, g_ref, wup_ref, cw_ref, cb_ref, wdn_ref, gf_ref, o_ref,
                hn_ref, a_ref, *, nc, final_norm):
    tm = x_ref.shape[1]
    dff = wdn_ref.shape[0]
    sub = min(SUB_TILE, tm)
    _fill_normed_ext(hn_ref, x_ref[0], xp_ref[0], xn_ref[0], g_ref[...])
    nchunks = dff // nc
    for t in range(tm // sub):
        r0 = t * sub
        hn_ext = hn_ref[r0:r0 + sub + 2 * HALO, :]
        hn = hn_ref[HALO + r0:HALO + r0 + sub, :]

        def project(c):
            lo, hi = c * nc, (c + 1) * nc
            return _dot(hn, wup_ref[:, lo:hi]), _dot(hn_ext, wup_ref[:, dff + lo:dff + hi])

        cur = project(0)
        for c in range(nchunks):
            nxt = project(c + 1) if c + 1 < nchunks else None
            u, gp = cur
            lo, hi = c * nc, (c + 1) * nc
            gate = _conv3_rows(gp, cw_ref[:, lo:hi], sub) + cb_ref[:, lo:hi]
            a_ref[r0:r0 + sub, lo:hi] = (_gelu_exact(gate) * u).astype(BF16)
            cur = nxt
        acc = x_ref[0, r0:r0 + sub, :] + _dot(a_ref[r0:r0 + sub, :], wdn_ref[...])
        if final_norm:
            acc = _rms(acc, gf_ref[...])
        o_ref[0, r0:r0 + sub, :] = acc


def _conv_ffn(h, g, w_up, conv_w, conv_b, w_down, final_g, tm, nc=256):
    b, s, d = h.shape
    dff = w_down.shape[0]
    final_norm = final_g is not None
    gf = (final_g if final_norm else g).reshape(1, d)
    return pl.pallas_call(
        functools.partial(_ffn_kernel, nc=nc, final_norm=final_norm),
        grid=(b, s // tm),
        in_specs=_halo_specs(tm, s, d) + [
            _const_spec((1, d)),
            _const_spec(w_up.shape),
            _const_spec(conv_w.shape),
            _const_spec((1, dff)),
            _const_spec(w_down.shape),
            _const_spec((1, d)),
        ],
        out_specs=pl.BlockSpec((1, tm, d), lambda bi, i: (bi, i, 0)),
        out_shape=jax.ShapeDtypeStruct(h.shape, F32),
        scratch_shapes=[pltpu.VMEM((tm + 2 * HALO, d), BF16), pltpu.VMEM((tm, dff), BF16)],
        compiler_params=_params("arbitrary", "arbitrary"),
        name="conv_ffn",
    )(h, h, h, g.reshape(1, d), w_up, conv_w, conv_b.reshape(1, dff), w_down, gf)


def kernel(x, mem, mem_norm_g, mix_norm_g, w_in_ab, rpb, w_out_ab, w_in_c, conv_c, w_out_c,
           xa_norm_g, xa_wq, xa_wkv, xa_wo, ffn_norm_g, ffn_w_up, ffn_conv_w, ffn_conv_b,
           ffn_w_down, final_norm_g):
    b, s, d = x.shape
    depth = mix_norm_g.shape[0]
    n_mem = mem.shape[1]
    tile_ffn = min(TOKEN_TILE, s)
    tile_wide = min(2 * TOKEN_TILE, s)
    bf = lambda w: w.astype(BF16)

    cdft, m1, m3 = _dft_constants(s)
    mq, no = _mem_fold(mem.reshape(b * n_mem, d), mem_norm_g, bf(xa_wkv), bf(xa_wq), bf(xa_wo), b)

    h = x
    for layer in range(depth):
        j = layer // 2
        if layer % 2 == 0:
            g = mix_norm_g[layer]
            qkv, zt = _even_in(h, g, bf(w_in_ab[j]), cdft)
            yf = _fft_mix(zt, m1, m3)
            h = _na_mix(qkv, rpb[j], yf, h, bf(w_out_ab[j]))
        else:
            h = _odd_mixer(h, mix_norm_g[layer], bf(w_in_c[j]), conv_c[j], bf(w_out_c[j]), tile_wide)
        h = _xattn(h, xa_norm_g[layer], mq, no, layer, tile_wide)
        h = _conv_ffn(h, ffn_norm_g[layer], bf(ffn_w_up[layer]), ffn_conv_w[layer], ffn_conv_b[layer],
                      bf(ffn_w_down[layer]), final_norm_g if layer == depth - 1 else None, tile_ffn)
    return h
```

```python
import functools
import math

import numpy as np
import jax
import jax.numpy as jnp
from jax import lax
from jax.experimental import pallas as pl
from jax.experimental.pallas import tpu as pltpu

EPS = 1e-6
GRID_W = 64
NA_KH, NA_KW = 8, 16
NA_HEADS = 4
NA_HEAD_DIM = 128
FOURIER_GROUPS = 4
FOURIER_GROUP_DIM = 128
XA_HEADS = 4
NEG_MASK = -1e30

LANES = 128
SUBLANES = 8
HALO = 16
SUB_TILE = 512
TOKEN_TILE = 1024
VMEM_LIMIT = 56 * 1024 * 1024

F32 = jnp.float32
BF16 = jnp.bfloat16

FW = FOURIER_GROUPS * FOURIER_GROUP_DIM
NW = NA_HEADS * NA_HEAD_DIM
NI = GRID_W


def _rms(x, g):
    ms = jnp.mean(x * x, axis=-1, keepdims=True)
    return (x * lax.rsqrt(ms + EPS)) * g


def _dot(a, b):
    return jnp.dot(a, b, preferred_element_type=F32)


def _dot_nt(a, b):
    return lax.dot_general(a, b, (((1,), (1,)), ((), ())), preferred_element_type=F32)


LOG2E = 1.4426950408889634


def _softmax_pv(s, v):
    m = jnp.max(s, axis=-1, keepdims=True)
    p = jnp.exp2(s - m)
    l = jnp.sum(p, axis=-1, keepdims=True)
    return _dot(p.astype(BF16), v) / l


def _const_spec(shape):
    nd = len(shape)
    return pl.BlockSpec(shape, lambda *_: (0,) * nd, pipeline_mode=pl.Buffered(1))


def _params(*sem):
    return pltpu.CompilerParams(dimension_semantics=sem, vmem_limit_bytes=VMEM_LIMIT)


MEM_BATCHES = 2


def _mem_fold_kernel(m_ref, g_ref, wkv_ref, wq_ref, wo_ref, mq_ref, no_ref):
    d = m_ref.shape[1]
    n_mem = m_ref.shape[0] // MEM_BATCHES
    dh = d // XA_HEADS
    for t in range(MEM_BATCHES):
        mn = _rms(m_ref[t * n_mem:(t + 1) * n_mem, :], g_ref[...]).astype(BF16)
        kv = _dot(mn, wkv_ref[0]).astype(BF16)
        for hd in range(XA_HEADS):
            k = kv[:, hd * dh:(hd + 1) * dh]
            v = kv[:, d + hd * dh:d + (hd + 1) * dh]
            mq = _dot_nt(wq_ref[0, :, hd * dh:(hd + 1) * dh], k) * (LOG2E * dh ** -0.5)
            mq_ref[0, t, :, hd * n_mem:(hd + 1) * n_mem] = mq.astype(BF16)
            no_ref[0, t, hd * n_mem:(hd + 1) * n_mem, :] = _dot(
                v, wo_ref[0, hd * dh:(hd + 1) * dh, :]).astype(BF16)


def _mem_fold(mem2d, g, wkv, wq, wo, b):
    bm, d = mem2d.shape
    n_mem = bm // b
    nl = wkv.shape[0]
    hm = XA_HEADS * n_mem
    nb = MEM_BATCHES
    assert b % nb == 0
    return pl.pallas_call(
        _mem_fold_kernel,
        grid=(nl, b // nb),
        in_specs=[
            pl.BlockSpec((nb * n_mem, d), lambda l, bi: (bi, 0)),
            pl.BlockSpec((1, d), lambda l, bi: (0, 0)),
            pl.BlockSpec((1, d, 2 * d), lambda l, bi: (l, 0, 0)),
            pl.BlockSpec((1, d, d), lambda l, bi: (l, 0, 0)),
            pl.BlockSpec((1, d, d), lambda l, bi: (l, 0, 0)),
        ],
        out_specs=[pl.BlockSpec((1, nb, d, hm), lambda l, bi: (l, bi, 0, 0)),
                   pl.BlockSpec((1, nb, hm, d), lambda l, bi: (l, bi, 0, 0))],
        out_shape=[jax.ShapeDtypeStruct((nl, b, d, hm), BF16), jax.ShapeDtypeStruct((nl, b, hm, d), BF16)],
        compiler_params=_params("arbitrary", "arbitrary"),
        name="mem_fold",
    )(mem2d, g.reshape(1, d), wkv, wq, wo)


EIN_I = 16
EIN_CHAINS = 4


def _even_in_kernel(x_ref, g_ref, w_ref, cdft_ref, qkv_ref, zt_ref, sc_ref):
    no, d = x_ref.shape[1], x_ref.shape[3]
    gd = FOURIER_GROUP_DIM
    nslab = 2 * FOURIER_GROUPS
    ngrp8 = EIN_I // SUBLANES
    oc = no // EIN_CHAINS
    cd = cdft_ref[...]
    for ch in range(EIN_CHAINS):
        o0 = ch * oc
        x = x_ref[0, o0:o0 + oc].reshape(oc * EIN_I, d)
        hn = _rms(x, g_ref[...]).astype(BF16)
        z = _dot(hn, w_ref[...])
        q = (z[:, FW:FW + NW] * (LOG2E * NA_HEAD_DIM ** -0.5)).astype(BF16)
        qkv = jnp.concatenate([q, z[:, FW + NW:].astype(BF16)], axis=1)
        qkv_ref[0, o0:o0 + oc] = qkv.reshape(oc, EIN_I, 3 * NW)
        for grp in range(FOURIER_GROUPS):
            pq = _dot(z[:, grp * gd:(grp + 1) * gd].astype(BF16), cd)
            pq = pq.reshape(oc, EIN_I, 2 * gd)
            for g8 in range(ngrp8):
                part = pq[:, g8 * SUBLANES:(g8 + 1) * SUBLANES, :].reshape(oc * SUBLANES, 2 * gd)
                rows = pl.ds(o0 * SUBLANES, oc * SUBLANES)
                sc_ref[g8, grp, rows, :] = part[:, :gd]
                sc_ref[g8, FOURIER_GROUPS + grp, rows, :] = part[:, gd:]
    for g8 in range(ngrp8):
        for il in range(SUBLANES):
            rows = [sc_ref[g8, sl, pl.ds(il, no, stride=SUBLANES), :] for sl in range(nslab)]
            zt_ref[0, g8 * SUBLANES + il] = jnp.concatenate(rows, axis=1).astype(BF16)


def _even_in(h, g, w_in, cdft):
    b, s, d = h.shape
    no = s // NI
    h4 = h.reshape(b, no, NI, d)
    qkv, zt = pl.pallas_call(
        _even_in_kernel,
        grid=(b, NI // EIN_I),
        in_specs=[
            pl.BlockSpec((1, no, EIN_I, d), lambda bi, i: (bi, 0, i, 0)),
            _const_spec((1, d)),
            _const_spec(w_in.shape),
            _const_spec(cdft.shape),
        ],
        out_specs=[pl.BlockSpec((1, no, EIN_I, 3 * NW), lambda bi, i: (bi, 0, i, 0)),
                   pl.BlockSpec((1, EIN_I, no, 2 * FW), lambda bi, i: (bi, i, 0, 0))],
        out_shape=[jax.ShapeDtypeStruct((b, no, NI, 3 * NW), BF16),
                   jax.ShapeDtypeStruct((b, NI, no, 2 * FW), BF16)],
        scratch_shapes=[pltpu.VMEM((EIN_I // SUBLANES, 2 * FW // LANES, no * SUBLANES, LANES), F32)],
        compiler_params=_params("arbitrary", "arbitrary"),
        name="even_in_proj",
    )(h4, g.reshape(1, d), w_in, cdft)
    return qkv.reshape(b, s, 3 * NW), zt


FFT_OUT_STEPS = 2


def _fft_kernel(zt_ref, m1_ref, m3_ref, o_ref, bt_ref, yf_ref, *, pitch_b, pitch_y):
    no = zt_ref.shape[2]
    nslab = FW // LANES
    j = pl.program_id(1)

    @pl.when(j == 0)
    def _():
        for i in range(NI):
            zi = zt_ref[0, i]
            rhs = jnp.concatenate([zi[:, :FW], zi[:, FW:]], axis=0)
            a = _dot(m1_ref[i], rhs)
            for sl in range(nslab):
                bt_ref[sl, i * pitch_b:i * pitch_b + 2 * no, :] = a[:, sl * LANES:(sl + 1) * LANES]

        def stage2(k1, carry):
            br = [bt_ref[sl, pl.ds(k1, NI, stride=pitch_b), :] for sl in range(nslab)]
            bi = [bt_ref[sl, pl.ds(no + k1, NI, stride=pitch_b), :] for sl in range(nslab)]
            rhs = jnp.concatenate([jnp.concatenate(br, axis=1).astype(BF16),
                                   jnp.concatenate(bi, axis=1).astype(BF16)], axis=0)
            y = _dot(m3_ref[...], rhs)
            for sl in range(nslab):
                yf_ref[sl, pl.ds(k1, NI, stride=pitch_y), :] = y[:, sl * LANES:(sl + 1) * LANES]
            return carry

        lax.fori_loop(0, no, stage2, 0, unroll=16)

    k2_step = NI // FFT_OUT_STEPS
    for jj in range(FFT_OUT_STEPS):
        @pl.when(j == jj)
        def _():
            for k2l in range(k2_step):
                row0 = (jj * k2_step + k2l) * pitch_y
                rows = jnp.concatenate([yf_ref[sl, row0:row0 + no, :] for sl in range(nslab)], axis=1)
                o_ref[0, k2l * no:(k2l + 1) * no, :] = rows.astype(BF16)


def _fft_mix(zt, m1, m3):
    b, _, no, _ = zt.shape
    s = NI * no
    tm = s // FFT_OUT_STEPS
    assert no % SUBLANES == 0
    pitch_b = 2 * no + SUBLANES // 2
    pitch_y = no + SUBLANES // 2
    return pl.pallas_call(
        functools.partial(_fft_kernel, pitch_b=pitch_b, pitch_y=pitch_y),
        grid=(b, FFT_OUT_STEPS),
        in_specs=[
            pl.BlockSpec((1, NI, no, 2 * FW), lambda bi, j: (jnp.minimum(bi + (j > 0), b - 1), 0, 0, 0)),
            _const_spec(m1.shape),
            _const_spec(m3.shape),
        ],
        out_specs=pl.BlockSpec((1, tm, FW), lambda bi, j: (bi, j, 0)),
        out_shape=jax.ShapeDtypeStruct((b, s, FW), BF16),
        scratch_shapes=[
            pltpu.VMEM((FW // LANES, NI * pitch_b, LANES), F32),
            pltpu.VMEM((FW // LANES, NI * pitch_y, LANES), F32),
        ],
        compiler_params=_params("arbitrary", "arbitrary"),
        name="fft_mix",
    )(zt, m1, m3)


def _dft_constants(s):
    n = FOURIER_GROUP_DIM
    jn = jnp.arange(n, dtype=jnp.int32)
    ang = ((jn[:, None] * jn[None, :]) % n).astype(F32) * (2.0 * math.pi / n)
    cdft = (jnp.concatenate([jnp.cos(ang), jnp.sin(ang)], axis=1) * (n ** -0.5)).astype(BF16)
    no = s // NI
    i = jnp.arange(NI, dtype=jnp.int32)[:, None, None]
    k1 = jnp.arange(no, dtype=jnp.int32)[None, :, None]
    o = jnp.arange(no, dtype=jnp.int32)[None, None, :]
    ang1 = ((k1 * (NI * o + i)) % s).astype(F32) * (2.0 * math.pi / s)
    c1, s1 = jnp.cos(ang1), jnp.sin(ang1)
    m1 = jnp.concatenate([jnp.concatenate([c1, -s1], axis=2),
                          jnp.concatenate([-s1, -c1], axis=2)], axis=1).astype(BF16)
    ji = jnp.arange(NI, dtype=jnp.int32)
    ang3 = ((ji[:, None] * ji[None, :]) % NI).astype(F32) * (2.0 * math.pi / NI)
    m3 = (jnp.concatenate([jnp.cos(ang3), jnp.sin(ang3)], axis=1) * (s ** -0.5)).astype(BF16)
    return cdft, m1, m3


NA_QROWS = 16


def _na_kernel(q_ref, k_ref, v_ref, bias_ref, yf_ref, x_ref, w_ref, o_ref, ya_ref, s_ref, *, rows):
    blk = pl.program_id(1)
    nk = NA_KH * GRID_W
    dh = NA_HEAD_DIM
    kstarts, deltas = [], []
    for a in range(NA_QROWS):
        i = blk * NA_QROWS + a
        rs = jnp.clip(i - NA_KH // 2, 0, rows - NA_KH)
        kstarts.append(pl.multiple_of(rs * GRID_W, GRID_W))
        deltas.append(i - rs)
    for a in range(NA_QROWS):
        for hd in range(NA_HEADS):
            q = q_ref[0, a * GRID_W:(a + 1) * GRID_W, hd * dh:(hd + 1) * dh]
            k = k_ref[0, pl.ds(kstarts[a], nk), hd * dh:(hd + 1) * dh]
            s_ref[a * NA_HEADS + hd] = _dot_nt(q, k) + bias_ref[deltas[a], hd]
    for a in range(NA_QROWS):
        for hd in range(NA_HEADS):
            v = v_ref[0, pl.ds(kstarts[a], nk), hd * dh:(hd + 1) * dh]
            ya_ref[a * GRID_W:(a + 1) * GRID_W, hd * dh:(hd + 1) * dh] = _softmax_pv(
                s_ref[a * NA_HEADS + hd], v).astype(BF16)
    o_ref[0] = x_ref[0] + _dot(yf_ref[0], w_ref[:FW, :]) + _dot(ya_ref[...], w_ref[FW:, :])


def _na_bias_table(rpb):
    cols = np.arange(GRID_W)
    cs = np.clip(cols - NA_KW // 2, 0, GRID_W - NA_KW)
    col_valid = (cols[None, :] >= cs[:, None]) & (cols[None, :] < cs[:, None] + NA_KW)
    col_off = np.clip(cols[None, :] - cols[:, None] + NA_KW - 1, 0, 2 * NA_KW - 2)
    delta = np.arange(NA_KH)
    row_off = np.arange(NA_KH)[None, :] - delta[:, None] + NA_KH - 1
    sel_r = (row_off[:, :, None] == np.arange(2 * NA_KH - 1)).astype(np.float32)
    sel_c = (col_off[:, :, None] == np.arange(2 * NA_KW - 1)).astype(np.float32)
    bias = jnp.einsum("hok,dro,jck->dhjrc", rpb, sel_r, sel_c, precision=lax.Precision.HIGHEST)
    bias = jnp.where(col_valid[None, None, :, None, :], bias * LOG2E, NEG_MASK)
    return bias.reshape(NA_KH, rpb.shape[0], GRID_W, NA_KH * GRID_W).astype(F32)


def _na_mix(qkv, rpb, yf, h, w_out):
    b, s, d = h.shape
    rows = s // GRID_W
    assert rows >= NA_KH and rows % NA_QROWS == 0
    table = _na_bias_table(rpb)
    tq = NA_QROWS * GRID_W
    nk = NA_KH * GRID_W
    return pl.pallas_call(
        functools.partial(_na_kernel, rows=rows),
        grid=(b, rows // NA_QROWS),
        in_specs=[
            pl.BlockSpec((1, tq, NW), lambda bi, i: (bi, i, 0)),
            pl.BlockSpec((1, s, NW), lambda bi, i: (bi, 0, 1)),
            pl.BlockSpec((1, s, NW), lambda bi, i: (bi, 0, 2)),
            _const_spec(table.shape),
            pl.BlockSpec((1, tq, FW), lambda bi, i: (bi, i, 0)),
            pl.BlockSpec((1, tq, d), lambda bi, i: (bi, i, 0)),
            _const_spec(w_out.shape),
        ],
        out_specs=pl.BlockSpec((1, tq, d), lambda bi, i: (bi, i, 0)),
        out_shape=jax.ShapeDtypeStruct(h.shape, F32),
        scratch_shapes=[pltpu.VMEM((tq, NW), BF16), pltpu.VMEM((NA_QROWS * NA_HEADS, GRID_W, nk), F32)],
        compiler_params=_params("arbitrary", "arbitrary"),
        name="neighborhood_attn",
    )(qkv, qkv, qkv, table, yf, h, w_out)


def _halo_specs(tm, s, d):
    r = tm // HALO
    last_blk = s // HALO - 1

    def prev_map(b, i):
        return (b, jnp.maximum(i * r - 1, 0), 0)

    def next_map(b, i):
        return (b, jnp.minimum((i + 1) * r, last_blk), 0)

    return [
        pl.BlockSpec((1, tm, d), lambda b, i: (b, i, 0)),
        pl.BlockSpec((1, HALO, d), prev_map),
        pl.BlockSpec((1, HALO, d), next_map),
    ]


def _fill_normed_ext(hn_ref, x, xp, xn, g):
    tm = x.shape[0]
    i = pl.program_id(1)
    first = i == 0
    last = i == pl.num_programs(1) - 1
    hp = jnp.where(first, 0.0, _rms(xp, g))
    hx = jnp.where(last, 0.0, _rms(xn, g))
    hn_ref[:HALO, :] = hp.astype(BF16)
    hn_ref[HALO:HALO + tm, :] = _rms(x, g).astype(BF16)
    hn_ref[HALO + tm:, :] = hx.astype(BF16)


def _conv3_rows(z, cw, tm):
    n = z.shape[0]
    prev = pltpu.roll(z, 1, 0)[HALO:HALO + tm]
    nxt = pltpu.roll(z, n - 1, 0)[HALO:HALO + tm]
    return prev * cw[0:1, :] + z[HALO:HALO + tm] * cw[1:2, :] + nxt * cw[2:3, :]


def _odd_mixer_kernel(x_ref, xp_ref, xn_ref, g_ref, win_ref, cw_ref, wout_ref, o_ref, hn_ref, y_ref, *, nc):
    tm, d = x_ref.shape[1], x_ref.shape[2]
    sub = min(SUB_TILE, tm)
    _fill_normed_ext(hn_ref, x_ref[0], xp_ref[0], xn_ref[0], g_ref[...])
    nchunks = d // nc
    for t in range(tm // sub):
        r0 = t * sub
        hn_ext = hn_ref[r0:r0 + sub + 2 * HALO, :]
        hn = hn_ref[HALO + r0:HALO + r0 + sub, :]

        def project(c):
            lo, hi = c * nc, (c + 1) * nc
            return (_dot(hn, win_ref[:, lo:hi]),
                    _dot(hn_ext, win_ref[:, d + lo:d + hi]),
                    _dot(hn_ext, win_ref[:, 2 * d + lo:2 * d + hi]))

        cur = project(0)
        for c in range(nchunks):
            nxt = project(c + 1) if c + 1 < nchunks else None
            gate_b, gate_c, u = cur
            lo, hi = c * nc, (c + 1) * nc
            y_ref[r0:r0 + sub, lo:hi] = (gate_b * _conv3_rows(gate_c * u, cw_ref[:, lo:hi], sub)).astype(BF16)
            cur = nxt
        o_ref[0, r0:r0 + sub, :] = x_ref[0, r0:r0 + sub, :] + _dot(y_ref[r0:r0 + sub, :], wout_ref[...])


def _odd_mixer(h, g, w_in, conv_w, w_out, tm, nc=256):
    b, s, d = h.shape
    return pl.pallas_call(
        functools.partial(_odd_mixer_kernel, nc=nc),
        grid=(b, s // tm),
        in_specs=_halo_specs(tm, s, d) + [
            _const_spec((1, d)),
            _const_spec(w_in.shape),
            _const_spec(conv_w.shape),
            _const_spec(w_out.shape),
        ],
        out_specs=pl.BlockSpec((1, tm, d), lambda bi, i: (bi, i, 0)),
        out_shape=jax.ShapeDtypeStruct(h.shape, F32),
        scratch_shapes=[pltpu.VMEM((tm + 2 * HALO, d), BF16), pltpu.VMEM((tm, d), BF16)],
        compiler_params=_params("arbitrary", "arbitrary"),
        name="odd_mixer",
    )(h, h, h, g.reshape(1, d), w_in, conv_w, w_out)


def _xattn_kernel(x_ref, g_ref, mq_ref, no_ref, o_ref, p_ref, s_ref):
    n_mem = mq_ref.shape[3] // XA_HEADS
    tm = x_ref.shape[1]
    sub = min(SUB_TILE, tm)
    for t in range(tm // sub):
        r0 = t * sub
        slot = t % 2
        x = x_ref[0, r0:r0 + sub, :]
        hn = _rms(x, g_ref[...]).astype(BF16)
        s_ref[slot] = _dot(hn, mq_ref[0, 0])
        for hd in range(XA_HEADS):
            s = s_ref[slot, :, hd * n_mem:(hd + 1) * n_mem]
            p = jnp.exp2(s - jnp.max(s, axis=-1, keepdims=True))
            r = 1.0 / jnp.sum(p, axis=-1, keepdims=True)
            p_ref[slot, :, hd * n_mem:(hd + 1) * n_mem] = (p * r).astype(BF16)
        o_ref[0, r0:r0 + sub, :] = x + _dot(p_ref[slot], no_ref[0, 0])


def _xattn(h, g, mq, no, layer, tm):
    b, s, d = h.shape
    hm = mq.shape[3]
    return pl.pallas_call(
        _xattn_kernel,
        grid=(b, s // tm),
        in_specs=[
            pl.BlockSpec((1, tm, d), lambda bi, i: (bi, i, 0)),
            _const_spec((1, d)),
            pl.BlockSpec((1, 1, d, hm), lambda bi, i: (layer, bi, 0, 0)),
            pl.BlockSpec((1, 1, hm, d), lambda bi, i: (layer, bi, 0, 0)),
        ],
        out_specs=pl.BlockSpec((1, tm, d), lambda bi, i: (bi, i, 0)),
        out_shape=jax.ShapeDtypeStruct(h.shape, F32),
        scratch_shapes=[pltpu.VMEM((2, min(SUB_TILE, tm), hm), BF16), pltpu.VMEM((2, min(SUB_TILE, tm), hm), F32)],
        compiler_params=_params("arbitrary", "arbitrary"),
        name="mem_xattn",
    )(h, g.reshape(1, d), mq, no)


def _gelu_exact(x):
    return 0.5 * x * (1.0 + lax.erf(x * (1.0 / math.sqrt(2.0))))


def _ffn_kernel(x_ref, xp_ref, xn_ref, g_ref, wup_ref, cw_ref, cb_ref, wdn_ref, gf_ref, o_ref,
                hn_ref, a_ref, *, nc, final_norm):
    tm = x_ref.shape[1]
    dff = wdn_ref.shape[0]
    sub = min(SUB_TILE, tm)
    _fill_normed_ext(hn_ref, x_ref[0], xp_ref[0], xn_ref[0], g_ref[...])
    nchunks = dff // nc
    for t in range(tm // sub):
        r0 = t * sub
        hn_ext = hn_ref[r0:r0 + sub + 2 * HALO, :]
        hn = hn_ref[HALO + r0:HALO + r0 + sub, :]

        def project(c):
            lo, hi = c * nc, (c + 1) * nc
            return _dot(hn, wup_ref[:, lo:hi]), _dot(hn_ext, wup_ref[:, dff + lo:dff + hi])

        cur = project(0)
        for c in range(nchunks):
            nxt = project(c + 1) if c + 1 < nchunks else None
            u, gp = cur
            lo, hi = c * nc, (c + 1) * nc
            gate = _conv3_rows(gp, cw_ref[:, lo:hi], sub) + cb_ref[:, lo:hi]
            a_ref[r0:r0 + sub, lo:hi] = (_gelu_exact(gate) * u).astype(BF16)
            cur = nxt
        acc = x_ref[0, r0:r0 + sub, :] + _dot(a_ref[r0:r0 + sub, :], wdn_ref[...])
        if final_norm:
            acc = _rms(acc, gf_ref[...])
        o_ref[0, r0:r0 + sub, :] = acc


def _conv_ffn(h, g, w_up, conv_w, conv_b, w_down, final_g, tm, nc=256):
    b, s, d = h.shape
    dff = w_down.shape[0]
    final_norm = final_g is not None
    gf = (final_g if final_norm else g).reshape(1, d)
    return pl.pallas_call(
        functools.partial(_ffn_kernel, nc=nc, final_norm=final_norm),
        grid=(b, s // tm),
        in_specs=_halo_specs(tm, s, d) + [
            _const_spec((1, d)),
            _const_spec(w_up.shape),
            _const_spec(conv_w.shape),
            _const_spec((1, dff)),
            _const_spec(w_down.shape),
            _const_spec((1, d)),
        ],
        out_specs=pl.BlockSpec((1, tm, d), lambda bi, i: (bi, i, 0)),
        out_shape=jax.ShapeDtypeStruct(h.shape, F32),
        scratch_shapes=[pltpu.VMEM((tm + 2 * HALO, d), BF16), pltpu.VMEM((tm, dff), BF16)],
        compiler_params=_params("arbitrary", "arbitrary"),
        name="conv_ffn",
    )(h, h, h, g.reshape(1, d), w_up, conv_w, conv_b.reshape(1, dff), w_down, gf)


def kernel(x, mem, mem_norm_g, mix_norm_g, w_in_ab, rpb, w_out_ab, w_in_c, conv_c, w_out_c,
           xa_norm_g, xa_wq, xa_wkv, xa_wo, ffn_norm_g, ffn_w_up, ffn_conv_w, ffn_conv_b,
           ffn_w_down, final_norm_g):
    b, s, d = x.shape
    depth = mix_norm_g.shape[0]
    n_mem = mem.shape[1]
    tile_ffn = min(TOKEN_TILE, s)
    tile_wide = min(2 * TOKEN_TILE, s)
    bf = lambda w: w.astype(BF16)

    cdft, m1, m3 = _dft_constants(s)
    mq, no = _mem_fold(mem.reshape(b * n_mem, d), mem_norm_g, bf(xa_wkv), bf(xa_wq), bf(xa_wo), b)

    h = x
    for layer in range(depth):
        j = layer // 2
        if layer % 2 == 0:
            g = mix_norm_g[layer]
            qkv, zt = _even_in(h, g, bf(w_in_ab[j]), cdft)
            yf = _fft_mix(zt, m1, m3)
            h = _na_mix(qkv, rpb[j], yf, h, bf(w_out_ab[j]))
        else:
            h = _odd_mixer(h, mix_norm_g[layer], bf(w_in_c[j]), conv_c[j], bf(w_out_c[j]), tile_wide)
        h = _xattn(h, xa_norm_g[layer], mq, no, layer, tile_wide)
        h = _conv_ffn(h, ffn_norm_g[layer], bf(ffn_w_up[layer]), ffn_conv_w[layer], ffn_conv_b[layer],
                      bf(ffn_w_down[layer]), final_norm_g if layer == depth - 1 else None, tile_ffn)
    return h
```

```python
import functools
import math

import numpy as np
import jax
import jax.numpy as jnp
from jax import lax
from jax.experimental import pallas as pl
from jax.experimental.pallas import tpu as pltpu

EPS = 1e-6
GRID_W = 64
NA_KH, NA_KW = 8, 16
NA_HEADS = 4
NA_HEAD_DIM = 128
FOURIER_GROUPS = 4
FOURIER_GROUP_DIM = 128
XA_HEADS = 4
NEG_MASK = -1e30

LANES = 128
SUBLANES = 8
HALO = 16
SUB_TILE = 512
TOKEN_TILE = 1024
VMEM_LIMIT = 56 * 1024 * 1024

F32 = jnp.float32
BF16 = jnp.bfloat16

FW = FOURIER_GROUPS * FOURIER_GROUP_DIM
NW = NA_HEADS * NA_HEAD_DIM
NI = GRID_W


def _rms(x, g):
    ms = jnp.mean(x * x, axis=-1, keepdims=True)
    return (x * lax.rsqrt(ms + EPS)) * g


def _dot(a, b):
    return jnp.dot(a, b, preferred_element_type=F32)


def _dot_nt(a, b):
    return lax.dot_general(a, b, (((1,), (1,)), ((), ())), preferred_element_type=F32)


LOG2E = 1.4426950408889634


def _softmax_pv(s, v):
    m = jnp.max(s, axis=-1, keepdims=True)
    p = jnp.exp2(s - m)
    l = jnp.sum(p, axis=-1, keepdims=True)
    return _dot(p.astype(BF16), v) / l


def _const_spec(shape):
    nd = len(shape)
    return pl.BlockSpec(shape, lambda *_: (0,) * nd, pipeline_mode=pl.Buffered(1))


def _params(*sem):
    return pltpu.CompilerParams(dimension_semantics=sem, vmem_limit_bytes=VMEM_LIMIT)


MEM_BATCHES = 2


def _mem_fold_kernel(m_ref, g_ref, wkv_ref, wq_ref, wo_ref, mq_ref, no_ref):
    d = m_ref.shape[1]
    n_mem = m_ref.shape[0] // MEM_BATCHES
    dh = d // XA_HEADS
    for t in range(MEM_BATCHES):
        mn = _rms(m_ref[t * n_mem:(t + 1) * n_mem, :], g_ref[...]).astype(BF16)
        kv = _dot(mn, wkv_ref[0]).astype(BF16)
        for hd in range(XA_HEADS):
            k = kv[:, hd * dh:(hd + 1) * dh]
            v = kv[:, d + hd * dh:d + (hd + 1) * dh]
            mq = _dot_nt(wq_ref[0, :, hd * dh:(hd + 1) * dh], k) * (LOG2E * dh ** -0.5)
            mq_ref[0, t, :, hd * n_mem:(hd + 1) * n_mem] = mq.astype(BF16)
            no_ref[0, t, hd * n_mem:(hd + 1) * n_mem, :] = _dot(
                v, wo_ref[0, hd * dh:(hd + 1) * dh, :]).astype(BF16)


def _mem_fold(mem2d, g, wkv, wq, wo, b):
    bm, d = mem2d.shape
    n_mem = bm // b
    nl = wkv.shape[0]
    hm = XA_HEADS * n_mem
    nb = MEM_BATCHES
    assert b % nb == 0
    return pl.pallas_call(
        _mem_fold_kernel,
        grid=(nl, b // nb),
        in_specs=[
            pl.BlockSpec((nb * n_mem, d), lambda l, bi: (bi, 0)),
            pl.BlockSpec((1, d), lambda l, bi: (0, 0)),
            pl.BlockSpec((1, d, 2 * d), lambda l, bi: (l, 0, 0)),
            pl.BlockSpec((1, d, d), lambda l, bi: (l, 0, 0)),
            pl.BlockSpec((1, d, d), lambda l, bi: (l, 0, 0)),
        ],
        out_specs=[pl.BlockSpec((1, nb, d, hm), lambda l, bi: (l, bi, 0, 0)),
                   pl.BlockSpec((1, nb, hm, d), lambda l, bi: (l, bi, 0, 0))],
        out_shape=[jax.ShapeDtypeStruct((nl, b, d, hm), BF16), jax.ShapeDtypeStruct((nl, b, hm, d), BF16)],
        compiler_params=_params("arbitrary", "arbitrary"),
        name="mem_fold",
    )(mem2d, g.reshape(1, d), wkv, wq, wo)


EIN_I = 16
EIN_CHAINS = 4


def _even_in_kernel(x_ref, g_ref, w_ref, cdft_ref, qkv_ref, zt_ref, sc_ref):
    no, d = x_ref.shape[1], x_ref.shape[3]
    gd = FOURIER_GROUP_DIM
    nslab = 2 * FOURIER_GROUPS
    ngrp8 = EIN_I // SUBLANES
    oc = no // EIN_CHAINS
    cd = cdft_ref[...]
    for ch in range(EIN_CHAINS):
        o0 = ch * oc
        x = x_ref[0, o0:o0 + oc].reshape(oc * EIN_I, d)
        hn = _rms(x, g_ref[...]).astype(BF16)
        z = _dot(hn, w_ref[...])
        q = (z[:, FW:FW + NW] * (LOG2E * NA_HEAD_DIM ** -0.5)).astype(BF16)
        qkv = jnp.concatenate([q, z[:, FW + NW:].astype(BF16)], axis=1)
        qkv_ref[0, o0:o0 + oc] = qkv.reshape(oc, EIN_I, 3 * NW)
        for grp in range(FOURIER_GROUPS):
            pq = _dot(z[:, grp * gd:(grp + 1) * gd].astype(BF16), cd)
            pq = pq.reshape(oc, EIN_I, 2 * gd)
            for g8 in range(ngrp8):
                part = pq[:, g8 * SUBLANES:(g8 + 1) * SUBLANES, :].reshape(oc * SUBLANES, 2 * gd)
                rows = pl.ds(o0 * SUBLANES, oc * SUBLANES)
                sc_ref[g8, grp, rows, :] = part[:, :gd]
                sc_ref[g8, FOURIER_GROUPS + grp, rows, :] = part[:, gd:]
    for g8 in range(ngrp8):
        for il in range(SUBLANES):
            rows = [sc_ref[g8, sl, pl.ds(il, no, stride=SUBLANES), :] for sl in range(nslab)]
            zt_ref[0, g8 * SUBLANES + il] = jnp.concatenate(rows, axis=1).astype(BF16)


def _even_in(h, g, w_in, cdft):
    b, s, d = h.shape
    no = s // NI
    h4 = h.reshape(b, no, NI, d)
    qkv, zt = pl.pallas_call(
        _even_in_kernel,
        grid=(b, NI // EIN_I),
        in_specs=[
            pl.BlockSpec((1, no, EIN_I, d), lambda bi, i: (bi, 0, i, 0)),
            _const_spec((1, d)),
            _const_spec(w_in.shape),
            _const_spec(cdft.shape),
        ],
        out_specs=[pl.BlockSpec((1, no, EIN_I, 3 * NW), lambda bi, i: (bi, 0, i, 0)),
                   pl.BlockSpec((1, EIN_I, no, 2 * FW), lambda bi, i: (bi, i, 0, 0))],
        out_shape=[jax.ShapeDtypeStruct((b, no, NI, 3 * NW), BF16),
                   jax.ShapeDtypeStruct((b, NI, no, 2 * FW), BF16)],
        scratch_shapes=[pltpu.VMEM((EIN_I // SUBLANES, 2 * FW // LANES, no * SUBLANES, LANES), F32)],
        compiler_params=_params("arbitrary", "arbitrary"),
        name="even_in_proj",
    )(h4, g.reshape(1, d), w_in, cdft)
    return qkv.reshape(b, s, 3 * NW), zt


FFT_OUT_STEPS = 2


def _fft_kernel(zt_ref, m1_ref, m3_ref, o_ref, bt_ref, yf_ref, *, pitch_b, pitch_y):
    no = zt_ref.shape[2]
    nslab = FW // LANES
    j = pl.program_id(1)

    @pl.when(j == 0)
    def _():
        for i in range(NI):
            zi = zt_ref[0, i]
            rhs = jnp.concatenate([zi[:, :FW], zi[:, FW:]], axis=0)
            a = _dot(m1_ref[i], rhs)
            for sl in range(nslab):
                bt_ref[sl, i * pitch_b:i * pitch_b + 2 * no, :] = a[:, sl * LANES:(sl + 1) * LANES]

        def stage2(k1, carry):
            br = [bt_ref[sl, pl.ds(k1, NI, stride=pitch_b), :] for sl in range(nslab)]
            bi = [bt_ref[sl, pl.ds(no + k1, NI, stride=pitch_b), :] for sl in range(nslab)]
            rhs = jnp.concatenate([jnp.concatenate(br, axis=1).astype(BF16),
                                   jnp.concatenate(bi, axis=1).astype(BF16)], axis=0)
            y = _dot(m3_ref[...], rhs)
            for sl in range(nslab):
                yf_ref[sl, pl.ds(k1, NI, stride=pitch_y), :] = y[:, sl * LANES:(sl + 1) * LANES]
            return carry

        lax.fori_loop(0, no, stage2, 0, unroll=16)

    k2_step = NI // FFT_OUT_STEPS
    for jj in range(FFT_OUT_STEPS):
        @pl.when(j == jj)
        def _():
            for k2l in range(k2_step):
                row0 = (jj * k2_step + k2l) * pitch_y
                rows = jnp.concatenate([yf_ref[sl, row0:row0 + no, :] for sl in range(nslab)], axis=1)
                o_ref[0, k2l * no:(k2l + 1) * no, :] = rows.astype(BF16)


def _fft_mix(zt, m1, m3):
    b, _, no, _ = zt.shape
    s = NI * no
    tm = s // FFT_OUT_STEPS
    assert no % SUBLANES == 0
    pitch_b = 2 * no + SUBLANES // 2
    pitch_y = no + SUBLANES // 2
    return pl.pallas_call(
        functools.partial(_fft_kernel, pitch_b=pitch_b, pitch_y=pitch_y),
        grid=(b, FFT_OUT_STEPS),
        in_specs=[
            pl.BlockSpec((1, NI, no, 2 * FW), lambda bi, j: (jnp.minimum(bi + (j > 0), b - 1), 0, 0, 0)),
            _const_spec(m1.shape),
            _const_spec(m3.shape),
        ],
        out_specs=pl.BlockSpec((1, tm, FW), lambda bi, j: (bi, j, 0)),
        out_shape=jax.ShapeDtypeStruct((b, s, FW), BF16),
        scratch_shapes=[
            pltpu.VMEM((FW // LANES, NI * pitch_b, LANES), F32),
            pltpu.VMEM((FW // LANES, NI * pitch_y, LANES), F32),
        ],
        compiler_params=_params("arbitrary", "arbitrary"),
        name="fft_mix",
    )(zt, m1, m3)


def _dft_constants(s):
    n = FOURIER_GROUP_DIM
    jn = np.arange(n)
    ang = ((jn[:, None] * jn[None, :]) % n) * (2.0 * math.pi / n)
    cdft = np.concatenate([np.cos(ang), np.sin(ang)], axis=1) * (n ** -0.5)
    no = s // NI
    i = np.arange(NI)[:, None, None]
    k1 = np.arange(no)[None, :, None]
    o = np.arange(no)[None, None, :]
    ang1 = ((k1 * (NI * o + i)) % s) * (2.0 * math.pi / s)
    c1, s1 = np.cos(ang1), np.sin(ang1)
    m1 = np.concatenate([np.concatenate([c1, -s1], axis=2), np.concatenate([-s1, -c1], axis=2)], axis=1)
    ji = np.arange(NI)
    ang3 = ((ji[:, None] * ji[None, :]) % NI) * (2.0 * math.pi / NI)
    m3 = np.concatenate([np.cos(ang3), np.sin(ang3)], axis=1) * (s ** -0.5)
    return tuple(jnp.asarray(c.astype(np.float32).astype(BF16)) for c in (cdft, m1, m3))


NA_QROWS = 16


def _na_kernel(q_ref, k_ref, v_ref, bias_ref, yf_ref, x_ref, w_ref, o_ref, ya_ref, s_ref, *, rows):
    blk = pl.program_id(1)
    nk = NA_KH * GRID_W
    dh = NA_HEAD_DIM
    kstarts, deltas = [], []
    for a in range(NA_QROWS):
        i = blk * NA_QROWS + a
        rs = jnp.clip(i - NA_KH // 2, 0, rows - NA_KH)
        kstarts.append(pl.multiple_of(rs * GRID_W, GRID_W))
        deltas.append(i - rs)
    for a in range(NA_QROWS):
        for hd in range(NA_HEADS):
            q = q_ref[0, a * GRID_W:(a + 1) * GRID_W, hd * dh:(hd + 1) * dh]
            k = k_ref[0, pl.ds(kstarts[a], nk), hd * dh:(hd + 1) * dh]
            s_ref[a * NA_HEADS + hd] = _dot_nt(q, k) + bias_ref[deltas[a], hd]
    for a in range(NA_QROWS):
        for hd in range(NA_HEADS):
            v = v_ref[0, pl.ds(kstarts[a], nk), hd * dh:(hd + 1) * dh]
            ya_ref[a * GRID_W:(a + 1) * GRID_W, hd * dh:(hd + 1) * dh] = _softmax_pv(
                s_ref[a * NA_HEADS + hd], v).astype(BF16)
    o_ref[0] = x_ref[0] + _dot(yf_ref[0], w_ref[:FW, :]) + _dot(ya_ref[...], w_ref[FW:, :])


def _na_bias_table(rpb):
    cols = np.arange(GRID_W)
    cs = np.clip(cols - NA_KW // 2, 0, GRID_W - NA_KW)
    col_valid = (cols[None, :] >= cs[:, None]) & (cols[None, :] < cs[:, None] + NA_KW)
    col_off = np.clip(cols[None, :] - cols[:, None] + NA_KW - 1, 0, 2 * NA_KW - 2)
    delta = np.arange(NA_KH)
    row_off = np.arange(NA_KH)[None, :] - delta[:, None] + NA_KH - 1
    sel_r = (row_off[:, :, None] == np.arange(2 * NA_KH - 1)).astype(np.float32)
    sel_c = (col_off[:, :, None] == np.arange(2 * NA_KW - 1)).astype(np.float32)
    bias = jnp.einsum("hok,dro,jck->dhjrc", rpb, sel_r, sel_c, precision=lax.Precision.HIGHEST)
    bias = jnp.where(col_valid[None, None, :, None, :], bias * LOG2E, NEG_MASK)
    return bias.reshape(NA_KH, rpb.shape[0], GRID_W, NA_KH * GRID_W).astype(F32)


def _na_mix(qkv, rpb, yf, h, w_out):
    b, s, d = h.shape
    rows = s // GRID_W
    assert rows >= NA_KH and rows % NA_QROWS == 0
    table = _na_bias_table(rpb)
    tq = NA_QROWS * GRID_W
    nk = NA_KH * GRID_W
    return pl.pallas_call(
        functools.partial(_na_kernel, rows=rows),
        grid=(b, rows // NA_QROWS),
        in_specs=[
            pl.BlockSpec((1, tq, NW), lambda bi, i: (bi, i, 0)),
            pl.BlockSpec((1, s, NW), lambda bi, i: (bi, 0, 1)),
            pl.BlockSpec((1, s, NW), lambda bi, i: (bi, 0, 2)),
            _const_spec(table.shape),
            pl.BlockSpec((1, tq, FW), lambda bi, i: (bi, i, 0)),
            pl.BlockSpec((1, tq, d), lambda bi, i: (bi, i, 0)),
            _const_spec(w_out.shape),
        ],
        out_specs=pl.BlockSpec((1, tq, d), lambda bi, i: (bi, i, 0)),
        out_shape=jax.ShapeDtypeStruct(h.shape, F32),
        scratch_shapes=[pltpu.VMEM((tq, NW), BF16), pltpu.VMEM((NA_QROWS * NA_HEADS, GRID_W, nk), F32)],
        compiler_params=_params("arbitrary", "arbitrary"),
        name="neighborhood_attn",
    )(qkv, qkv, qkv, table, yf, h, w_out)


def _halo_specs(tm, s, d):
    r = tm // HALO
    last_blk = s // HALO - 1

    def prev_map(b, i):
        return (b, jnp.maximum(i * r - 1, 0), 0)

    def next_map(b, i):
        return (b, jnp.minimum((i + 1) * r, last_blk), 0)

    return [
        pl.BlockSpec((1, tm, d), lambda b, i: (b, i, 0)),
        pl.BlockSpec((1, HALO, d), prev_map),
        pl.BlockSpec((1, HALO, d), next_map),
    ]


def _fill_normed_ext(hn_ref, x, xp, xn, g):
    tm = x.shape[0]
    i = pl.program_id(1)
    first = i == 0
    last = i == pl.num_programs(1) - 1
    hp = jnp.where(first, 0.0, _rms(xp, g))
    hx = jnp.where(last, 0.0, _rms(xn, g))
    hn_ref[:HALO, :] = hp.astype(BF16)
    hn_ref[HALO:HALO + tm, :] = _rms(x, g).astype(BF16)
    hn_ref[HALO + tm:, :] = hx.astype(BF16)


def _conv3_rows(z, cw, tm):
    n = z.shape[0]
    prev = pltpu.roll(z, 1, 0)[HALO:HALO + tm]
    nxt = pltpu.roll(z, n - 1, 0)[HALO:HALO + tm]
    return prev * cw[0:1, :] + z[HALO:HALO + tm] * cw[1:2, :] + nxt * cw[2:3, :]


def _odd_mixer_kernel(x_ref, xp_ref, xn_ref, g_ref, win_ref, cw_ref, wout_ref, o_ref, hn_ref, y_ref, *, nc):
    tm, d = x_ref.shape[1], x_ref.shape[2]
    sub = min(SUB_TILE, tm)
    _fill_normed_ext(hn_ref, x_ref[0], xp_ref[0], xn_ref[0], g_ref[...])
    nchunks = d // nc
    for t in range(tm // sub):
        r0 = t * sub
        hn_ext = hn_ref[r0:r0 + sub + 2 * HALO, :]
        hn = hn_ref[HALO + r0:HALO + r0 + sub, :]

        def project(c):
            lo, hi = c * nc, (c + 1) * nc
            return (_dot(hn, win_ref[:, lo:hi]),
                    _dot(hn_ext, win_ref[:, d + lo:d + hi]),
                    _dot(hn_ext, win_ref[:, 2 * d + lo:2 * d + hi]))

        cur = project(0)
        for c in range(nchunks):
            nxt = project(c + 1) if c + 1 < nchunks else None
            gate_b, gate_c, u = cur
            lo, hi = c * nc, (c + 1) * nc
            y_ref[r0:r0 + sub, lo:hi] = (gate_b * _conv3_rows(gate_c * u, cw_ref[:, lo:hi], sub)).astype(BF16)
            cur = nxt
        o_ref[0, r0:r0 + sub, :] = x_ref[0, r0:r0 + sub, :] + _dot(y_ref[r0:r0 + sub, :], wout_ref[...])


def _odd_mixer(h, g, w_in, conv_w, w_out, tm, nc=256):
    b, s, d = h.shape
    return pl.pallas_call(
        functools.partial(_odd_mixer_kernel, nc=nc),
        grid=(b, s // tm),
        in_specs=_halo_specs(tm, s, d) + [
            _const_spec((1, d)),
            _const_spec(w_in.shape),
            _const_spec(conv_w.shape),
            _const_spec(w_out.shape),
        ],
        out_specs=pl.BlockSpec((1, tm, d), lambda bi, i: (bi, i, 0)),
        out_shape=jax.ShapeDtypeStruct(h.shape, F32),
        scratch_shapes=[pltpu.VMEM((tm + 2 * HALO, d), BF16), pltpu.VMEM((tm, d), BF16)],
        compiler_params=_params("arbitrary", "arbitrary"),
        name="odd_mixer",
    )(h, h, h, g.reshape(1, d), w_in, conv_w, w_out)


def _xattn_kernel(x_ref, g_ref, mq_ref, no_ref, o_ref, p_ref, s_ref):
    n_mem = mq_ref.shape[3] // XA_HEADS
    tm = x_ref.shape[1]
    sub = min(SUB_TILE, tm)
    for t in range(tm // sub):
        r0 = t * sub
        slot = t % 2
        x = x_ref[0, r0:r0 + sub, :]
        hn = _rms(x, g_ref[...]).astype(BF16)
        s_ref[slot] = _dot(hn, mq_ref[0, 0])
        for hd in range(XA_HEADS):
            s = s_ref[slot, :, hd * n_mem:(hd + 1) * n_mem]
            p = jnp.exp2(s - jnp.max(s, axis=-1, keepdims=True))
            r = 1.0 / jnp.sum(p, axis=-1, keepdims=True)
            p_ref[slot, :, hd * n_mem:(hd + 1) * n_mem] = (p * r).astype(BF16)
        o_ref[0, r0:r0 + sub, :] = x + _dot(p_ref[slot], no_ref[0, 0])


def _xattn(h, g, mq, no, layer, tm):
    b, s, d = h.shape
    hm = mq.shape[3]
    return pl.pallas_call(
        _xattn_kernel,
        grid=(b, s // tm),
        in_specs=[
            pl.BlockSpec((1, tm, d), lambda bi, i: (bi, i, 0)),
            _const_spec((1, d)),
            pl.BlockSpec((1, 1, d, hm), lambda bi, i: (layer, bi, 0, 0)),
            pl.BlockSpec((1, 1, hm, d), lambda bi, i: (layer, bi, 0, 0)),
        ],
        out_specs=pl.BlockSpec((1, tm, d), lambda bi, i: (bi, i, 0)),
        out_shape=jax.ShapeDtypeStruct(h.shape, F32),
        scratch_shapes=[pltpu.VMEM((2, min(SUB_TILE, tm), hm), BF16), pltpu.VMEM((2, min(SUB_TILE, tm), hm), F32)],
        compiler_params=_params("arbitrary", "arbitrary"),
        name="mem_xattn",
    )(h, g.reshape(1, d), mq, no)


def _gelu_exact(x):
    return 0.5 * x * (1.0 + lax.erf(x * (1.0 / math.sqrt(2.0))))


def _ffn_kernel(x_ref, xp_ref, xn_ref, g_ref, wup_ref, cw_ref, cb_ref, wdn_ref, gf_ref, o_ref,
                hn_ref, a_ref, *, nc, final_norm):
    tm = x_ref.shape[1]
    dff = wdn_ref.shape[0]
    sub = min(SUB_TILE, tm)
    _fill_normed_ext(hn_ref, x_ref[0], xp_ref[0], xn_ref[0], g_ref[...])
    nchunks = dff // nc
    for t in range(tm // sub):
        r0 = t * sub
        hn_ext = hn_ref[r0:r0 + sub + 2 * HALO, :]
        hn = hn_ref[HALO + r0:HALO + r0 + sub, :]

        def project(c):
            lo, hi = c * nc, (c + 1) * nc
            return _dot(hn, wup_ref[:, lo:hi]), _dot(hn_ext, wup_ref[:, dff + lo:dff + hi])

        cur = project(0)
        for c in range(nchunks):
            nxt = project(c + 1) if c + 1 < nchunks else None
            u, gp = cur
            lo, hi = c * nc, (c + 1) * nc
            gate = _conv3_rows(gp, cw_ref[:, lo:hi], sub) + cb_ref[:, lo:hi]
            a_ref[r0:r0 + sub, lo:hi] = (_gelu_exact(gate) * u).astype(BF16)
            cur = nxt
        acc = x_ref[0, r0:r0 + sub, :] + _dot(a_ref[r0:r0 + sub, :], wdn_ref[...])
        if final_norm:
            acc = _rms(acc, gf_ref[...])
        o_ref[0, r0:r0 + sub, :] = acc


def _conv_ffn(h, g, w_up, conv_w, conv_b, w_down, final_g, tm, nc=256):
    b, s, d = h.shape
    dff = w_down.shape[0]
    final_norm = final_g is not None
    gf = (final_g if final_norm else g).reshape(1, d)
    return pl.pallas_call(
        functools.partial(_ffn_kernel, nc=nc, final_norm=final_norm),
        grid=(b, s // tm),
        in_specs=_halo_specs(tm, s, d) + [
            _const_spec((1, d)),
            _const_spec(w_up.shape),
            _const_spec(conv_w.shape),
            _const_spec((1, dff)),
            _const_spec(w_down.shape),
            _const_spec((1, d)),
        ],
        out_specs=pl.BlockSpec((1, tm, d), lambda bi, i: (bi, i, 0)),
        out_shape=jax.ShapeDtypeStruct(h.shape, F32),
        scratch_shapes=[pltpu.VMEM((tm + 2 * HALO, d), BF16), pltpu.VMEM((tm, dff), BF16)],
        compiler_params=_params("arbitrary", "arbitrary"),
        name="conv_ffn",
    )(h, h, h, g.reshape(1, d), w_up, conv_w, conv_b.reshape(1, dff), w_down, gf)


def kernel(x, mem, mem_norm_g, mix_norm_g, w_in_ab, rpb, w_out_ab, w_in_c, conv_c, w_out_c,
           xa_norm_g, xa_wq, xa_wkv, xa_wo, ffn_norm_g, ffn_w_up, ffn_conv_w, ffn_conv_b,
           ffn_w_down, final_norm_g):
    b, s, d = x.shape
    depth = mix_norm_g.shape[0]
    n_mem = mem.shape[1]
    tile_ffn = min(TOKEN_TILE, s)
    tile_wide = min(2 * TOKEN_TILE, s)
    bf = lambda w: w.astype(BF16)

    cdft, m1, m3 = _dft_constants(s)
    mq, no = _mem_fold(mem.reshape(b * n_mem, d), mem_norm_g, bf(xa_wkv), bf(xa_wq), bf(xa_wo), b)

    h = x
    for layer in range(depth):
        j = layer // 2
        if layer % 2 == 0:
            g = mix_norm_g[layer]
            qkv, zt = _even_in(h, g, bf(w_in_ab[j]), cdft)
            yf = _fft_mix(zt, m1, m3)
            h = _na_mix(qkv, rpb[j], yf, h, bf(w_out_ab[j]))
        else:
            h = _odd_mixer(h, mix_norm_g[layer], bf(w_in_c[j]), conv_c[j], bf(w_out_c[j]), tile_wide)
        h = _xattn(h, xa_norm_g[layer], mq, no, layer, tile_wide)
        h = _conv_ffn(h, ffn_norm_g[layer], bf(ffn_w_up[layer]), ffn_conv_w[layer], ffn_conv_b[layer],
                      bf(ffn_w_down[layer]), final_norm_g if layer == depth - 1 else None, tile_ffn)
    return h
```
